```python
import math
import jax, jax.numpy as jnp
from jax import lax
import numpy as np

D_MODEL = 4096
BATCH = 2
SEQ = 4096
DEPTH = 1
DEC_BATCH = 128
DEC_SEQ = 4
PAST_LEN = 2048
PAGE_SIZE = 128

HEAD_DIM = 128
D_MIX = D_MODEL
D_ATTN = D_MIX // 2
N_HEADS = D_ATTN // HEAD_DIM
D_CONV = D_MIX - D_ATTN
CONV_W = 3
D_IN = 4 * D_ATTN + N_HEADS + 4 * D_CONV
Q_BLOCK = 128
EPS = 1e-6
FORGET_BIAS_INIT = 3.0

kernel_name = "fox_shortconv_hybrid_step"


def rmsnorm(x, g):
    xf = x.astype(jnp.float32)
    y = xf * lax.rsqrt(jnp.mean(xf * xf, axis=-1, keepdims=True) + EPS) * g.astype(jnp.float32)
    return y.astype(x.dtype)


def in_proj(x, g_pre, w_in, b_f):
    b, t = x.shape[0], x.shape[1]
    xn = rmsnorm(x, g_pre)
    z = jnp.einsum('btd,de->bte', xn, w_in)
    sizes = [D_ATTN, D_ATTN, D_ATTN, N_HEADS, D_ATTN, D_CONV, D_CONV, D_CONV, D_CONV]
    idx = [int(i) for i in np.cumsum(sizes)[:-1]]
    q, k, v, f, g_a, b_gate, c_gate, h, g_c = jnp.split(z, idx, axis=-1)
    heads = lambda a: a.reshape(b, t, N_HEADS, HEAD_DIM)
    logf = jax.nn.log_sigmoid(f.astype(jnp.float32) + b_f.astype(jnp.float32))
    u = c_gate * h
    return heads(q), heads(k), heads(v), logf, g_a, b_gate, u, g_c


def short_conv(u, hist, w_conv):
    full = jnp.concatenate([hist.astype(u.dtype), u], axis=1)
    t = u.shape[1]
    y = sum(full[:, j:j + t] * w_conv[j] for j in range(CONV_W))
    return y, full[:, -(CONV_W - 1):]


def out_proj(x, o_attn, g_a, y_conv, b_gate, g_c, w_out, g_post):
    b, t = x.shape[0], x.shape[1]
    m = jnp.concatenate([o_attn.reshape(b, t, D_ATTN) * jax.nn.silu(g_a),
                         b_gate * y_conv * jax.nn.silu(g_c)], axis=-1)
    return x + rmsnorm(jnp.einsum('bte,ed->btd', m, w_out), g_post)


def fox_attention_prompt(q, k, v, logf):
    b, s_len = q.shape[0], q.shape[1]
    scale = HEAD_DIM ** -0.5
    c = jnp.cumsum(logf, axis=1)
    c_k = c.transpose(0, 2, 1)
    nb = s_len // Q_BLOCK
    qb = q.reshape(b, nb, Q_BLOCK, N_HEADS, HEAD_DIM).swapaxes(0, 1)
    cb = c.reshape(b, nb, Q_BLOCK, N_HEADS).swapaxes(0, 1)
    kpos = jnp.arange(s_len)

    def block(args):
        qi, ci, i = args
        s = jnp.einsum('bqhd,bkhd->bhqk', qi, k, preferred_element_type=jnp.float32) * scale
        s = s + ci.transpose(0, 2, 1)[..., None] - c_k[:, :, None, :]
        qpos = i * Q_BLOCK + jnp.arange(Q_BLOCK)
        s = jnp.where(kpos[None, :] <= qpos[:, None], s, -jnp.inf)
        p = jax.nn.softmax(s, axis=-1)
        return jnp.einsum('bhqk,bkhd->bqhd', p.astype(v.dtype), v)

    o = lax.map(block, (qb, cb, jnp.arange(nb)))
    return o.swapaxes(0, 1).reshape(b, s_len, N_HEADS, HEAD_DIM)


def fox_attention_sample(q, k, v, logf, cache_k, cache_v, cache_logf, page_table):
    scale = HEAD_DIM ** -0.5
    n_pages = PAST_LEN // PAGE_SIZE
    past = n_pages * PAGE_SIZE

    def one_seq(args):
        qi, ki, vi, li, pages = args
        t = qi.shape[0]
        ka = jnp.concatenate([cache_k[pages].reshape(past, N_HEADS, HEAD_DIM).astype(ki.dtype), ki], axis=0)
        va = jnp.concatenate([cache_v[pages].reshape(past, N_HEADS, HEAD_DIM).astype(vi.dtype), vi], axis=0)
        la = jnp.concatenate([cache_logf[pages].reshape(past, N_HEADS).astype(jnp.float32), li], axis=0)
        c = jnp.cumsum(la, axis=0)
        s = jnp.einsum('qhd,khd->hqk', qi, ka, preferred_element_type=jnp.float32) * scale
        s = s + c[past:].T[:, :, None] - c.T[:, None, :]
        mask = jnp.arange(past + t)[None, :] <= past + jnp.arange(t)[:, None]
        p = jax.nn.softmax(jnp.where(mask, s, -jnp.inf), axis=-1)
        return jnp.einsum('hqk,khd->qhd', p.astype(va.dtype), va)

    return lax.map(one_seq, (q, k, v, logf, page_table))


def setup_inputs(seed: int = 0) -> dict:
    key = jax.random.key(seed)
    ks = jax.random.split(key, 16)
    n_pages = PAST_LEN // PAGE_SIZE
    n_used = DEC_BATCH * n_pages
    n_pool = (n_used * 5) // 4
    f32 = jnp.float32
    x_prompt = jax.random.normal(ks[0], (BATCH, SEQ, D_MODEL), f32)
    x_sample = jax.random.normal(ks[1], (DEC_BATCH, DEC_SEQ, D_MODEL), f32)
    cache_k = jax.random.normal(ks[2], (DEPTH, n_pool, PAGE_SIZE, N_HEADS, HEAD_DIM), f32)
    cache_v = jax.random.normal(ks[3], (DEPTH, n_pool, PAGE_SIZE, N_HEADS, HEAD_DIM), f32)
    cache_logf = jax.nn.log_sigmoid(FORGET_BIAS_INIT + jax.random.normal(ks[4], (DEPTH, n_pool, PAGE_SIZE, N_HEADS), f32))
    state_conv = jax.random.normal(ks[5], (DEPTH, DEC_BATCH, CONV_W - 1, D_CONV), f32)
    page_table = jax.random.permutation(ks[6], n_pool)[:n_used].reshape(DEC_BATCH, n_pages).astype(jnp.int32)
    g_pre = 1.0 + 0.02 * jax.random.normal(ks[7], (DEPTH, D_MODEL), f32)
    w_in = jax.random.normal(ks[8], (DEPTH, D_MODEL, D_IN), f32) * D_MODEL ** -0.5
    b_f = FORGET_BIAS_INIT + 0.1 * jax.random.normal(ks[9], (DEPTH, N_HEADS), f32)
    w_conv = jax.random.normal(ks[10], (DEPTH, CONV_W, D_CONV), f32) * CONV_W ** -0.5
    w_out = jax.random.normal(ks[11], (DEPTH, D_MIX, D_MODEL), f32) * D_MIX ** -0.5
    g_post = 1.0 + 0.02 * jax.random.normal(ks[12], (DEPTH, D_MODEL), f32)
    return {"x_prompt": x_prompt, "x_sample": x_sample, "cache_k": cache_k, "cache_v": cache_v,
            "cache_logf": cache_logf, "state_conv": state_conv, "page_table": page_table,
            "g_pre": g_pre, "w_in": w_in, "b_f": b_f, "w_conv": w_conv, "w_out": w_out, "g_post": g_post}


def reference(x_prompt, x_sample, cache_k, cache_v, cache_logf, state_conv, page_table,
              g_pre, w_in, b_f, w_conv, w_out, g_post):
    xp, xs = x_prompt, x_sample
    kp_l, vp_l, lp_l, cp_l, ks_l, vs_l, ls_l, cs_l = [], [], [], [], [], [], [], []
    for layer in range(DEPTH):
        q, k, v, logf, g_a, b_gate, u, g_c = in_proj(xp, g_pre[layer], w_in[layer], b_f[layer])
        o_attn = fox_attention_prompt(q, k, v, logf)
        hist0 = jnp.zeros((xp.shape[0], CONV_W - 1, D_CONV), u.dtype)
        y_conv, conv_new = short_conv(u, hist0, w_conv[layer])
        xp_next = out_proj(xp, o_attn, g_a, y_conv, b_gate, g_c, w_out[layer], g_post[layer])
        kp_l.append(k); vp_l.append(v); lp_l.append(logf); cp_l.append(conv_new)
        q, k, v, logf, g_a, b_gate, u, g_c = in_proj(xs, g_pre[layer], w_in[layer], b_f[layer])
        o_attn = fox_attention_sample(q, k, v, logf, cache_k[layer], cache_v[layer],
                                      cache_logf[layer], page_table)
        y_conv, conv_new = short_conv(u, state_conv[layer], w_conv[layer])
        xs_next = out_proj(xs, o_attn, g_a, y_conv, b_gate, g_c, w_out[layer], g_post[layer])
        ks_l.append(k); vs_l.append(v); ls_l.append(logf); cs_l.append(conv_new)
        xp, xs = xp_next, xs_next
    k_prompt = jnp.stack(kp_l); v_prompt = jnp.stack(vp_l)
    logf_prompt = jnp.stack(lp_l); conv_prompt = jnp.stack(cp_l)
    k_sample = jnp.stack(ks_l); v_sample = jnp.stack(vs_l)
    logf_sample = jnp.stack(ls_l); conv_sample = jnp.stack(cs_l)
    return (xp, xs, k_prompt, v_prompt, logf_prompt, conv_prompt,
            k_sample, v_sample, logf_sample, conv_sample)
```

```python
import functools

import jax
import jax.numpy as jnp
from jax import lax
from jax.experimental import pallas as pl
from jax.experimental.pallas import tpu as pltpu

HEAD_DIM = 128
CONV_W = 3
EPS = 1e-6
LANES = 128
SUBLANES = 8
V7X_VMEM_BYTES = 64 * 1024 * 1024
VMEM_BUDGET_BYTES = 58 * 1024 * 1024
F32 = jnp.float32
BF16 = jnp.bfloat16
HIGHEST = lax.Precision.HIGHEST
NEG_INF = float("-inf")


def _params(vmem_bytes, n_grid):
    limit = min(int(vmem_bytes * 1.25) + (4 << 20), VMEM_BUDGET_BYTES)
    return pltpu.CompilerParams(dimension_semantics=("arbitrary",) * n_grid, vmem_limit_bytes=limit)


def _silu(z):
    return z * (1.0 / (1.0 + jnp.exp(-z)))


def _rmsnorm_kernel(x_ref, g_ref, o_ref):
    x = x_ref[...]
    ms = jnp.mean(x * x, axis=-1, keepdims=True)
    o_ref[...] = (x * lax.rsqrt(ms + EPS) * g_ref[...]).astype(o_ref.dtype)


def _rmsnorm_bf16(x2d, g_row):
    m, d = x2d.shape
    tm = min(256, m)
    return pl.pallas_call(
        _rmsnorm_kernel,
        grid=(m // tm,),
        in_specs=[pl.BlockSpec((tm, d), lambda i: (i, 0)), pl.BlockSpec((1, d), lambda i: (0, 0))],
        out_specs=pl.BlockSpec((tm, d), lambda i: (i, 0)),
        out_shape=jax.ShapeDtypeStruct((m, d), BF16),
        compiler_params=_params(2 * tm * d * 6, 1),
        name="rmsnorm_pre",
    )(x2d, g_row)


def _proj_call(body, xn, w, out_dtypes, tm, tn, name):
    m, d = xn.shape
    n = w.shape[1]
    out_bytes = sum(jnp.dtype(t).itemsize for t in out_dtypes)
    vmem = 2 * (tm * d * 2 + d * tn * 2 + tm * tn * out_bytes) + tm * tn * 4
    return pl.pallas_call(
        body,
        grid=(n // tn, m // tm),
        in_specs=[pl.BlockSpec((tm, d), lambda j, i: (i, 0)), pl.BlockSpec((d, tn), lambda j, i: (0, j))],
        out_specs=[pl.BlockSpec((tm, tn), lambda j, i: (i, j)) for _ in out_dtypes],
        out_shape=[jax.ShapeDtypeStruct((m, n), t) for t in out_dtypes],
        compiler_params=_params(vmem, 2),
        name=name,
    )(xn, w)


def _q_body(xn_ref, w_ref, o_ref, *, scale):
    z = jnp.dot(xn_ref[...], w_ref[...], preferred_element_type=F32)
    o_ref[...] = (z * scale).astype(o_ref.dtype)


def _kv_body(xn_ref, w_ref, o32_ref, o16_ref):
    z = jnp.dot(xn_ref[...], w_ref[...], preferred_element_type=F32)
    o32_ref[...] = z
    o16_ref[...] = z.astype(BF16)


def _gate_body(xn_ref, w_ref, o_ref):
    z = jnp.dot(xn_ref[...], w_ref[...], preferred_element_type=F32)
    o_ref[...] = _silu(z).astype(o_ref.dtype)


def _log_sigmoid(z):
    return jnp.minimum(z, 0.0) - jnp.log1p(jnp.exp(-jnp.abs(z)))


def _forget_body(xn_ref, w_ref, b_ref, logf_ref, c_ref, carry_ref, *, tiles_per_seq):
    i = pl.program_id(0)
    tm = xn_ref.shape[0]
    logf = _log_sigmoid(jnp.dot(xn_ref[...], w_ref[...], preferred_element_type=F32) + b_ref[...])
    logf_ref[...] = logf

    @pl.when(i % tiles_per_seq == 0)
    def _():
        carry_ref[...] = jnp.zeros_like(carry_ref)

    row = lax.broadcasted_iota(jnp.int32, (tm, tm), 0)
    col = lax.broadcasted_iota(jnp.int32, (tm, tm), 1)
    tri = (col <= row).astype(F32)
    c = jnp.dot(tri, logf, precision=HIGHEST, preferred_element_type=F32) + carry_ref[0:1, :]
    c_ref[...] = c
    carry_ref[...] = jnp.broadcast_to(c[tm - 1:tm, :], carry_ref.shape)


def _forget_call(xn, wf_pad, bf_pad, tiles_per_seq, tm):
    m, d = xn.shape
    vmem = 2 * (tm * d * 2 + d * LANES * 2 + 2 * tm * LANES * 4) + tm * tm * 8
    return pl.pallas_call(
        functools.partial(_forget_body, tiles_per_seq=tiles_per_seq),
        grid=(m // tm,),
        in_specs=[pl.BlockSpec((tm, d), lambda i: (i, 0)),
                  pl.BlockSpec((d, LANES), lambda i: (0, 0)),
                  pl.BlockSpec((1, LANES), lambda i: (0, 0))],
        out_specs=[pl.BlockSpec((tm, LANES), lambda i: (i, 0)), pl.BlockSpec((tm, LANES), lambda i: (i, 0))],
        out_shape=[jax.ShapeDtypeStruct((m, LANES), F32), jax.ShapeDtypeStruct((m, LANES), F32)],
        scratch_shapes=[pltpu.VMEM((SUBLANES, LANES), F32)],
        compiler_params=_params(vmem, 1),
        name="forget_gate",
    )(xn, wf_pad, bf_pad)


def _conv_tail(zb, zg, u, u1, u2, wc_ref, m_ref):
    wc = wc_ref[...]
    y = wc[0:1, :] * u2 + wc[1:2, :] * u1 + wc[2:3, :] * u
    m_ref[...] = (zb * y * _silu(zg)).astype(m_ref.dtype)


def _conv_prompt_body(xn_ref, w_ref, wc_ref, m_ref, cn_ref, carry_ref, *, tiles_per_seq, tc):
    i = pl.program_id(1)
    tm = xn_ref.shape[0]
    z = jnp.dot(xn_ref[...], w_ref[...], preferred_element_type=F32)
    zb, zc, zh, zg = (z[:, k * tc:(k + 1) * tc] for k in range(4))
    u = zc * zh

    @pl.when(i % tiles_per_seq == 0)
    def _():
        carry_ref[...] = jnp.zeros_like(carry_ref)

    prev = carry_ref[...]
    p1 = prev[SUBLANES - 1:SUBLANES, :]
    p2 = prev[SUBLANES - 2:SUBLANES - 1, :]
    row = lax.broadcasted_iota(jnp.int32, (tm, tc), 0)
    u1 = jnp.where(row >= 1, pltpu.roll(u, 1, axis=0), p1)
    u2 = jnp.where(row >= 2, pltpu.roll(u, 2, axis=0), jnp.where(row == 1, p1, p2))
    carry_ref[...] = u[tm - SUBLANES:tm, :]
    cn_ref[0] = u[tm - (CONV_W - 1):tm, :]
    _conv_tail(zb, zg, u, u1, u2, wc_ref, m_ref)


def _conv_sample_body(xn_ref, w_ref, wc_ref, h1_ref, h2_ref, m_ref, u_ref, *, dec_seq, tc):
    tm = xn_ref.shape[0]
    z = jnp.dot(xn_ref[...], w_ref[...], preferred_element_type=F32)
    zb, zc, zh, zg = (z[:, k * tc:(k + 1) * tc] for k in range(4))
    u = zc * zh
    u_ref[...] = u
    t = lax.rem(lax.broadcasted_iota(jnp.int32, (tm, tc), 0), dec_seq)
    u1 = jnp.where(t >= 1, pltpu.roll(u, 1, axis=0), h1_ref[...])
    u2 = jnp.where(t >= 2, pltpu.roll(u, 2, axis=0), h2_ref[...])
    _conv_tail(zb, zg, u, u1, u2, wc_ref, m_ref)


def _conv_prompt_call(xn, w4, wconv, batch, tm, tc):
    m, d = xn.shape
    dc = wconv.shape[1]
    tiles_per_seq = m // batch // tm
    vmem = 2 * (tm * d * 2 + d * 4 * tc * 2 + tm * tc * 2) + tm * 4 * tc * 4 * 2
    return pl.pallas_call(
        functools.partial(_conv_prompt_body, tiles_per_seq=tiles_per_seq, tc=tc),
        grid=(dc // tc, m // tm),
        in_specs=[pl.BlockSpec((tm, d), lambda j, i: (i, 0)),
                  pl.BlockSpec((d, 4 * tc), lambda j, i: (0, j)),
                  pl.BlockSpec((CONV_W, tc), lambda j, i: (0, j))],
        out_specs=[pl.BlockSpec((tm, tc), lambda j, i: (i, j)),
                   pl.BlockSpec((1, CONV_W - 1, tc), lambda j, i: (i // tiles_per_seq, 0, j))],
        out_shape=[jax.ShapeDtypeStruct((m, dc), BF16), jax.ShapeDtypeStruct((batch, CONV_W - 1, dc), F32)],
        scratch_shapes=[pltpu.VMEM((SUBLANES, tc), F32)],
        compiler_params=_params(vmem, 2),
        name="conv_prompt",
    )(xn, w4, wconv)


def _conv_sample_call(xn, w4, wconv, h1, h2, dec_seq, tc):
    m, d = xn.shape
    dc = wconv.shape[1]
    vmem = 2 * (m * d * 2 + d * 4 * tc * 2 + m * tc * 14) + m * 4 * tc * 4 * 2
    return pl.pallas_call(
        functools.partial(_conv_sample_body, dec_seq=dec_seq, tc=tc),
        grid=(dc // tc,),
        in_specs=[pl.BlockSpec((m, d), lambda j: (0, 0)),
                  pl.BlockSpec((d, 4 * tc), lambda j: (0, j)),
                  pl.BlockSpec((CONV_W, tc), lambda j: (0, j)),
                  pl.BlockSpec((m, tc), lambda j: (0, j)),
                  pl.BlockSpec((m, tc), lambda j: (0, j))],
        out_specs=[pl.BlockSpec((m, tc), lambda j: (0, j)), pl.BlockSpec((m, tc), lambda j: (0, j))],
        out_shape=[jax.ShapeDtypeStruct((m, dc), BF16), jax.ShapeDtypeStruct((m, dc), F32)],
        compiler_params=_params(vmem, 1),
        name="conv_sample",
    )(xn, w4, wconv, h1, h2)


def _out_body(ma_ref, mc_ref, wa_ref, wc_ref, x_ref, g_ref, o_ref, *, tn, n_tiles):
    j = pl.program_id(1)
    z = (jnp.dot(ma_ref[...], wa_ref[...], preferred_element_type=F32)
         + jnp.dot(mc_ref[...], wc_ref[...], preferred_element_type=F32))
    o_ref[:, pl.ds(pl.multiple_of(j * tn, tn), tn)] = z

    @pl.when(j == n_tiles - 1)
    def _():
        zz = o_ref[...]
        ms = jnp.mean(zz * zz, axis=-1, keepdims=True)
        o_ref[...] = x_ref[...] + zz * lax.rsqrt(ms + EPS) * g_ref[...]


def _out_call(ma, mc, wo, x2d, g_row, tm, tn):
    m, da = ma.shape
    dcv = mc.shape[1]
    d = x2d.shape[1]
    assert da == dcv, "w_out row blocks are indexed in units of the attention width"
    n_tiles = d // tn
    vmem = 2 * (2 * tm * d * 4 + tm * (da + dcv) * 2 + (da + dcv) * tn * 2) + tm * tn * 4
    return pl.pallas_call(
        functools.partial(_out_body, tn=tn, n_tiles=n_tiles),
        grid=(m // tm, n_tiles),
        in_specs=[pl.BlockSpec((tm, da), lambda i, j: (i, 0)),
                  pl.BlockSpec((tm, dcv), lambda i, j: (i, 0)),
                  pl.BlockSpec((da, tn), lambda i, j: (0, j)),
                  pl.BlockSpec((dcv, tn), lambda i, j: (1, j)),
                  pl.BlockSpec((tm, d), lambda i, j: (i, 0)),
                  pl.BlockSpec((1, d), lambda i, j: (0, 0))],
        out_specs=pl.BlockSpec((tm, d), lambda i, j: (i, 0)),
        out_shape=jax.ShapeDtypeStruct((m, d), F32),
        compiler_params=_params(vmem, 2),
        name="out_proj",
    )(ma, mc, wo, wo, x2d, g_row)


def _attn_prompt_body(q_ref, k_ref, v_ref, ck_ref, cq_ref, sg_ref, o_ref, *, blk):
    qi = pl.program_id(2)
    q = q_ref[0]
    cq = cq_ref[0, 0]

    def tile(kj, carry, diagonal):
        m, l, acc = carry
        ks = pl.multiple_of(kj * blk, blk)
        k = k_ref[0, pl.ds(ks, blk), :]
        v = v_ref[0, pl.ds(ks, blk), :]
        s = lax.dot_general(q, k, (((1,), (1,)), ((), ())), preferred_element_type=F32)
        s = s - ck_ref[0, 0, :, pl.ds(ks, blk)]
        if diagonal:
            row = lax.broadcasted_iota(jnp.int32, (blk, blk), 0)
            col = lax.broadcasted_iota(jnp.int32, (blk, blk), 1)
            s = jnp.where(col <= row, s, NEG_INF)
        m_new = jnp.maximum(m, jnp.max(s, axis=-1, keepdims=True) + cq)
        alpha = jnp.exp(m - m_new)
        p = jnp.exp(s - (m_new - cq))
        l = alpha * l + jnp.sum(p, axis=-1, keepdims=True)
        acc = alpha * acc + jnp.dot(p.astype(BF16), v, preferred_element_type=F32)
        return m_new, l, acc

    init = (jnp.full((blk, 1), NEG_INF, F32), jnp.zeros((blk, 1), F32), jnp.zeros((blk, HEAD_DIM), F32))
    carry = lax.fori_loop(0, qi, lambda kj, c: tile(kj, c, False), init)
    _, l, acc = tile(qi, carry, True)
    o_ref[0] = (acc * (1.0 / l) * sg_ref[0].astype(F32)).astype(o_ref.dtype)


def _attn_prompt_call(q, k, v, ck_row, cq_col, sg, blk):
    b, s, da = q.shape
    h = da // HEAD_DIM
    vmem = 2 * (2 * s * HEAD_DIM * 2 + 3 * blk * HEAD_DIM * 2 + s * 4 * SUBLANES + blk * LANES * 4) + 6 * blk * blk * 4
    return pl.pallas_call(
        functools.partial(_attn_prompt_body, blk=blk),
        grid=(b, h, s // blk),
        in_specs=[pl.BlockSpec((1, blk, HEAD_DIM), lambda bi, hi, qi: (bi, qi, hi)),
                  pl.BlockSpec((1, s, HEAD_DIM), lambda bi, hi, qi: (bi, 0, hi)),
                  pl.BlockSpec((1, s, HEAD_DIM), lambda bi, hi, qi: (bi, 0, hi)),
                  pl.BlockSpec((1, 1, 1, s), lambda bi, hi, qi: (bi, hi, 0, 0)),
                  pl.BlockSpec((1, 1, blk, 1), lambda bi, hi, qi: (bi, hi, qi, 0)),
                  pl.BlockSpec((1, blk, HEAD_DIM), lambda bi, hi, qi: (bi, qi, hi))],
        out_specs=pl.BlockSpec((1, blk, HEAD_DIM), lambda bi, hi, qi: (bi, qi, hi)),
        out_shape=jax.ShapeDtypeStruct((b, s, da), BF16),
        compiler_params=_params(vmem, 3),
        name="attn_prompt",
    )(q, k, v, ck_row, cq_col, sg)


def _cum_body(pt_ref, *refs, n_pages, heads):
    del pt_ref
    x_refs, c_ref = refs[:n_pages + 1], refs[n_pages + 1]
    rows = x_refs[0].shape[-2]
    n_valid = (n_pages + 1) * rows
    n = -(-n_valid // LANES) * LANES
    blocks = [r[...] for r in x_refs]
    if n > n_valid:
        blocks.append(jnp.zeros((n - n_valid, LANES), F32))
    x = jnp.concatenate(blocks, axis=0)
    l1 = lax.broadcasted_iota(jnp.int32, (LANES, LANES), 0)
    l2 = lax.broadcasted_iota(jnp.int32, (LANES, LANES), 1)
    same_head = (l1 % heads) == (l2 % heads)
    within = (same_head & ((l1 // heads) <= (l2 // heads))).astype(F32)
    total = same_head.astype(F32)
    r1 = lax.broadcasted_iota(jnp.int32, (n, n), 0)
    r2 = lax.broadcasted_iota(jnp.int32, (n, n), 1)
    before = (r2 < r1).astype(F32)
    c = (jnp.dot(x, within, precision=HIGHEST, preferred_element_type=F32)
         + jnp.dot(before, jnp.dot(x, total, precision=HIGHEST, preferred_element_type=F32),
                   precision=HIGHEST, preferred_element_type=F32))
    c_ref[0] = c[:n_valid].reshape(n_pages + 1, rows, LANES)


def _cum_call(page_table, logf_pages, layer, new_rows, heads):
    db, n_pages = page_table.shape
    rows = logf_pages.shape[2]

    def page_map(p):
        return lambda s, pt: (layer, pt[s, p], 0, 0)

    in_specs = [pl.BlockSpec((None, None, rows, LANES), page_map(p)) for p in range(n_pages)]
    in_specs.append(pl.BlockSpec((None, rows, LANES), lambda s, pt: (s, 0, 0)))
    n = (n_pages + 1) * rows
    return pl.pallas_call(
        functools.partial(_cum_body, n_pages=n_pages, heads=heads),
        grid_spec=pltpu.PrefetchScalarGridSpec(
            num_scalar_prefetch=1, grid=(db,), in_specs=in_specs,
            out_specs=pl.BlockSpec((1, n_pages + 1, rows, LANES), lambda s, pt: (s, 0, 0, 0))),
        out_shape=jax.ShapeDtypeStruct((db, n_pages + 1, rows, LANES), F32),
        compiler_params=_params(4 * n * LANES * 4 + n * n * 8, 1),
        name="cache_logf_cumsum",
    )(page_table, *([logf_pages] * n_pages), new_rows)


def _attn_sample_body(pt_ref, *refs, group, heads, dec_seq, n_groups):
    del pt_ref
    k_refs, v_refs = refs[:group], refs[group:2 * group]
    q_ref, c_ref, cn_ref, cq_ref, kn_ref, vn_ref, sg_ref, o_ref, m_ref, l_ref, acc_ref = refs[2 * group:]
    g = pl.program_id(1)
    rows = heads * dec_seq
    q = q_ref[0]
    cq = cq_ref[0]

    @pl.when(g == 0)
    def _():
        m_ref[...] = jnp.full(m_ref.shape, NEG_INF, F32)
        l_ref[...] = jnp.zeros_like(l_ref)
        acc_ref[...] = jnp.zeros_like(acc_ref)

    r_head = lax.broadcasted_iota(jnp.int32, (rows, LANES), 0) % heads
    lane = lax.broadcasted_iota(jnp.int32, (rows, LANES), 1)
    same_head = (lane % heads) == r_head

    def update(kf, vf, bias_tiles):
        s = lax.dot_general(q, kf, (((1,), (1,)), ((), ())), preferred_element_type=F32)
        s = jnp.concatenate([s[:, t * LANES:(t + 1) * LANES] + bias_tiles[t] for t in range(len(bias_tiles))], axis=1)
        m = m_ref[...]
        m_new = jnp.maximum(m, jnp.max(s, axis=-1, keepdims=True) + cq)
        alpha = jnp.exp(m - m_new)
        p = jnp.exp(s - (m_new - cq))
        l_ref[...] = alpha * l_ref[...] + jnp.sum(p, axis=-1, keepdims=True)
        acc_ref[...] = alpha * acc_ref[...] + jnp.dot(p.astype(BF16), vf, preferred_element_type=F32)
        m_ref[...] = m_new

    head_mask = jnp.where(same_head, 0.0, NEG_INF).astype(F32)
    tiles_per_page = k_refs[0].shape[0] * heads // LANES
    for i in range(group):
        kf = k_refs[i][...].reshape(-1, HEAD_DIM).astype(BF16)
        vf = v_refs[i][...].reshape(-1, HEAD_DIM).astype(BF16)
        c = c_ref[0, i]
        update(kf, vf, [head_mask - c[t:t + 1, :] for t in range(tiles_per_page)])

    @pl.when(g == n_groups - 1)
    def _():
        r_query = lax.broadcasted_iota(jnp.int32, (rows, LANES), 0) // heads
        ok = same_head & ((lane // heads) <= r_query) & (lane < rows)
        bias = jnp.where(ok, 0.0, NEG_INF).astype(F32) - cn_ref[0, 0][0:1, :]
        update(kn_ref[0], vn_ref[0], [bias])
        o_ref[0] = acc_ref[...] * (1.0 / l_ref[...]) * sg_ref[0].astype(F32)


def _attn_sample_call(page_table, cache_k, cache_v, layer, q_rows, c_all, cq_col, kn_rows, vn_rows, sg_rows,
                      group):
    db, n_pages = page_table.shape
    _, _, page, heads, hd = cache_k.shape
    rows = q_rows.shape[1]
    dec_seq = rows // heads
    n_groups = n_pages // group
    tiles_per_page = page * heads // LANES

    def page_map(i):
        return lambda s, g, pt: (layer, pt[s, g * group + i], 0, 0, 0)

    kv_spec = [pl.BlockSpec((None, None, page, heads, hd), page_map(i)) for i in range(group)]
    in_specs = kv_spec + kv_spec + [
        pl.BlockSpec((1, rows, hd), lambda s, g, pt: (s, 0, 0)),
        pl.BlockSpec((1, group, tiles_per_page, LANES), lambda s, g, pt: (s, g, 0, 0)),
        pl.BlockSpec((1, 1, tiles_per_page, LANES), lambda s, g, pt: (s, n_pages, 0, 0)),
        pl.BlockSpec((1, rows, 1), lambda s, g, pt: (s, 0, 0)),
        pl.BlockSpec((1, LANES, hd), lambda s, g, pt: (s, 0, 0)),
        pl.BlockSpec((1, LANES, hd), lambda s, g, pt: (s, 0, 0)),
        pl.BlockSpec((1, rows, hd), lambda s, g, pt: (s, 0, 0)),
    ]
    vmem = 2 * 2 * group * page * heads * hd * 4 + 4 * rows * page * heads * 4 + (4 << 20)
    return pl.pallas_call(
        functools.partial(_attn_sample_body, group=group, heads=heads, dec_seq=dec_seq, n_groups=n_groups),
        grid_spec=pltpu.PrefetchScalarGridSpec(
            num_scalar_prefetch=1, grid=(db, n_groups), in_specs=in_specs,
            out_specs=pl.BlockSpec((1, rows, hd), lambda s, g, pt: (s, 0, 0)),
            scratch_shapes=[pltpu.VMEM((rows, 1), F32), pltpu.VMEM((rows, 1), F32), pltpu.VMEM((rows, hd), F32)]),
        out_shape=jax.ShapeDtypeStruct((db, rows, hd), F32),
        compiler_params=_params(vmem, 2),
        name="attn_sample",
    )(page_table, *([cache_k] * group), *([cache_v] * group), q_rows, c_all, c_all, cq_col, kn_rows, vn_rows,
      sg_rows)


def _pick(n, pref):
    t = min(pref, n)
    assert n % t == 0, (n, t)
    return t


def _layer(xp, xs, cache_k, cache_v, logf_pages, state_conv_l, page_table, layer,
           g_pre, w_in, b_f, w_conv, w_out, g_post):
    b, s, d = xp.shape
    db, ds, _ = xs.shape
    heads = b_f.shape[0]
    da = heads * HEAD_DIM
    dc = w_conv.shape[1]
    page = cache_k.shape[2]
    n_pages = page_table.shape[1]
    mp, msz = b * s, db * ds
    scale = HEAD_DIM ** -0.5

    wq, wk, wv = (w_in[:, i * da:(i + 1) * da].astype(BF16) for i in range(3))
    wf = jnp.pad(w_in[:, 3 * da:3 * da + heads], ((0, 0), (0, LANES - heads))).astype(BF16)
    bf = jnp.pad(b_f, (0, LANES - heads)).reshape(1, LANES).astype(F32)
    off = 3 * da + heads
    wga = w_in[:, off:off + da].astype(BF16)
    tc = _pick(dc, 256)
    w4 = (w_in[:, off + da:off + da + 4 * dc].astype(BF16)
          .reshape(d, 4, dc // tc, tc).transpose(0, 2, 1, 3).reshape(d, 4 * dc))
    wo = w_out.astype(BF16)
    gpre = g_pre.reshape(1, d)
    gpost = g_post.reshape(1, d)

    def project(x2d, tm):
        xn = _rmsnorm_bf16(x2d, gpre)
        tn = _pick(da, 1024)
        (qv,) = _proj_call(functools.partial(_q_body, scale=scale), xn, wq, [BF16], tm, tn, "proj_q")
        k32, k16 = _proj_call(_kv_body, xn, wk, [F32, BF16], tm, tn, "proj_k")
        v32, v16 = _proj_call(_kv_body, xn, wv, [F32, BF16], tm, tn, "proj_v")
        (sg,) = _proj_call(_gate_body, xn, wga, [BF16], tm, tn, "proj_gate")
        return xn, qv, k32, k16, v32, v16, sg

    xp2 = xp.reshape(mp, d)
    tm_p = _pick(s, 512)
    xn, qv, k32, k16, v32, v16, sg = project(xp2, tm_p)
    logf_pad, c_pad = _forget_call(xn, wf, bf, s // tm_p, tm_p)
    logf_p = logf_pad[:, :heads].reshape(b, s, heads)
    c_p = c_pad[:, :heads].reshape(b, s, heads)
    ck_row = c_p.transpose(0, 2, 1).reshape(b, heads, 1, s)
    cq_col = c_p.transpose(0, 2, 1).reshape(b, heads, s, 1)
    blk = _pick(s, 512)
    ma_p = _attn_prompt_call(qv.reshape(b, s, da), k16.reshape(b, s, da), v16.reshape(b, s, da),
                             ck_row, cq_col, sg.reshape(b, s, da), blk)
    mc_p, conv_p = _conv_prompt_call(xn, w4, w_conv, b, tm_p, tc)
    tm_o = _pick(s, 512)
    yp = _out_call(ma_p.reshape(mp, da), mc_p, wo, xp2, gpost, tm_o, _pick(d, 512)).reshape(b, s, d)
    k_p = k32.reshape(b, s, heads, HEAD_DIM)
    v_p = v32.reshape(b, s, heads, HEAD_DIM)

    xs2 = xs.reshape(msz, d)
    tm_s = _pick(msz, 512)
    xn, qv, k32, k16, v32, v16, sg = project(xs2, tm_s)
    logf_pad, _ = _forget_call(xn, wf, bf, 1, tm_s)
    logf_s = logf_pad[:, :heads].reshape(db, ds, heads)
    rows = ds * heads
    tiles_per_page = page * heads // LANES
    new_rows = jnp.pad(logf_s.reshape(db, 1, rows), ((0, 0), (0, tiles_per_page - 1), (0, LANES - rows)))
    c_all = _cum_call(page_table, logf_pages, layer, new_rows, heads)
    cq_s = c_all[:, n_pages, 0, :rows].reshape(db, rows, 1)
    pad_rows = ((0, 0), (0, LANES - rows), (0, 0))
    kn = jnp.pad(k16.reshape(db, rows, HEAD_DIM), pad_rows)
    vn = jnp.pad(v16.reshape(db, rows, HEAD_DIM), pad_rows)
    o_s = _attn_sample_call(page_table, cache_k, cache_v, layer, qv.reshape(db, rows, HEAD_DIM), c_all, cq_s,
                            kn, vn, sg.reshape(db, rows, HEAD_DIM), _pick(n_pages, 4))
    ma_s = o_s.reshape(msz, da).astype(BF16)
    hist = state_conv_l
    zeros = jnp.zeros((db, 1, dc), F32)
    h1 = jnp.concatenate([hist[:, 1:2]] + [zeros] * (ds - 1), axis=1).reshape(msz, dc)
    h2 = jnp.concatenate([hist[:, 0:1], hist[:, 1:2]] + [zeros] * (ds - 2), axis=1).reshape(msz, dc)
    mc_s, u_s = _conv_sample_call(xn, w4, w_conv, h1, h2, ds, tc)
    ys = _out_call(ma_s, mc_s, wo, xs2, gpost, _pick(msz, 512), _pick(d, 512)).reshape(db, ds, d)
    conv_s = u_s.reshape(db, ds, dc)[:, ds - (CONV_W - 1):]
    k_s = k32.reshape(db, ds, heads, HEAD_DIM)
    v_s = v32.reshape(db, ds, heads, HEAD_DIM)
    return yp, ys, (k_p, v_p, logf_p, conv_p, k_s, v_s, logf_s, conv_s)


def kernel(x_prompt, x_sample, cache_k, cache_v, cache_logf, state_conv, page_table, g_pre, w_in, b_f, w_conv,
           w_out, g_post):
    depth, n_pool, page, heads = cache_logf.shape
    assert cache_k.shape[-1] == HEAD_DIM and (page * heads) % LANES == 0 and LANES % heads == 0
    assert x_sample.shape[1] >= CONV_W - 1 and x_sample.shape[1] * heads <= LANES
    logf_pages = cache_logf.reshape(depth, n_pool, page * heads // LANES, LANES)
    xp, xs = x_prompt, x_sample
    per_layer = []
    for layer in range(depth):
        xp, xs, outs = _layer(xp, xs, cache_k, cache_v, logf_pages, state_conv[layer], page_table, layer,
                              g_pre[layer], w_in[layer], b_f[layer], w_conv[layer], w_out[layer], g_post[layer])
        per_layer.append(outs)
    stacked = [jnp.stack(leaf) for leaf in zip(*per_layer)]
    return (xp, xs, *stacked)
```

```python
import functools

import jax
import jax.numpy as jnp
from jax import lax
from jax.experimental import pallas as pl
from jax.experimental.pallas import tpu as pltpu

HEAD_DIM = 128
CONV_W = 3
EPS = 1e-6
LANES = 128
SUBLANES = 8
V7X_VMEM_BYTES = 64 * 1024 * 1024
VMEM_BUDGET_BYTES = 58 * 1024 * 1024
F32 = jnp.float32
BF16 = jnp.bfloat16
NEG_INF = float("-inf")
LOG2E = 1.4426950408889634


def _params(vmem_bytes, n_grid):
    limit = min(int(vmem_bytes * 1.25) + (4 << 20), VMEM_BUDGET_BYTES)
    return pltpu.CompilerParams(dimension_semantics=("arbitrary",) * n_grid, vmem_limit_bytes=limit)


def _silu(z):
    return z * (1.0 / (1.0 + jnp.exp(-z)))


def _split3(x):
    hi = x.astype(BF16)
    r = x - hi.astype(F32)
    mid = r.astype(BF16)
    lo = (r - mid.astype(F32)).astype(BF16)
    return hi, mid, lo


def _dot_f32_by_01(x, w01):
    w = w01.astype(BF16)
    hi, mid, lo = (jnp.dot(p, w, preferred_element_type=F32) for p in _split3(x))
    return (lo + mid) + hi


def _dot_01_by_f32(w01, x):
    w = w01.astype(BF16)
    hi, mid, lo = (jnp.dot(w, p, preferred_element_type=F32) for p in _split3(x))
    return (lo + mid) + hi


def _rmsnorm_kernel(x_ref, g_ref, o_ref):
    x = x_ref[...]
    ms = jnp.mean(x * x, axis=-1, keepdims=True)
    o_ref[...] = (x * lax.rsqrt(ms + EPS) * g_ref[...]).astype(o_ref.dtype)


def _rmsnorm_bf16(x2d, g_row):
    m, d = x2d.shape
    tm = min(256, m)
    return pl.pallas_call(
        _rmsnorm_kernel,
        grid=(m // tm,),
        in_specs=[pl.BlockSpec((tm, d), lambda i: (i, 0)), pl.BlockSpec((1, d), lambda i: (0, 0))],
        out_specs=pl.BlockSpec((tm, d), lambda i: (i, 0)),
        out_shape=jax.ShapeDtypeStruct((m, d), BF16),
        compiler_params=_params(2 * tm * d * 6, 1),
        name="rmsnorm_pre",
    )(x2d, g_row)


def _weights_body(w_ref, qkv_ref, f_ref, ga_ref, w4_ref, *, da, heads, dc, tc):
    qkv_ref[...] = w_ref[:, :3 * da].astype(BF16)
    lane = lax.broadcasted_iota(jnp.int32, f_ref.shape, 1)
    f_ref[...] = jnp.where(lane < heads, w_ref[:, 3 * da:3 * da + LANES], 0.0).astype(BF16)
    off = 3 * da + heads
    ga_ref[...] = w_ref[:, off:off + da].astype(BF16)
    for j in range(dc // tc):
        for grp in range(4):
            src = off + da + grp * dc + j * tc
            w4_ref[:, (4 * j + grp) * tc:(4 * j + grp + 1) * tc] = w_ref[:, src:src + tc].astype(BF16)


def _weights_call(w_in, heads, da, dc, tc):
    d, d_in = w_in.shape
    assert d_in == 4 * da + heads + 4 * dc
    tr = _pick(d, 128)
    widths = (3 * da, LANES, da, 4 * dc)
    vmem = 2 * tr * (d_in * 4 + sum(widths) * 2) + tr * d_in * 4
    return pl.pallas_call(
        functools.partial(_weights_body, da=da, heads=heads, dc=dc, tc=tc),
        grid=(d // tr,),
        in_specs=[pl.BlockSpec((tr, d_in), lambda i: (i, 0))],
        out_specs=[pl.BlockSpec((tr, n), lambda i: (i, 0)) for n in widths],
        out_shape=[jax.ShapeDtypeStruct((d, n), BF16) for n in widths],
        compiler_params=_params(vmem, 1),
        name="weights_bf16",
    )(w_in)


def _proj_call(body, xn, w, out_dtypes, tm, tn, name, n=None, col0=0):
    m, d = xn.shape
    n = w.shape[1] if n is None else n
    assert col0 % tn == 0 and n % tn == 0
    jb = col0 // tn
    out_bytes = sum(jnp.dtype(t).itemsize for t in out_dtypes)
    vmem = 2 * (tm * d * 2 + d * tn * 2 + tm * tn * out_bytes) + tm * tn * 4
    return pl.pallas_call(
        body,
        grid=(n // tn, m // tm),
        in_specs=[pl.BlockSpec((tm, d), lambda j, i: (i, 0)), pl.BlockSpec((d, tn), lambda j, i: (0, j + jb))],
        out_specs=[pl.BlockSpec((tm, tn), lambda j, i: (i, j)) for _ in out_dtypes],
        out_shape=[jax.ShapeDtypeStruct((m, n), t) for t in out_dtypes],
        compiler_params=_params(vmem, 2),
        name=name,
    )(xn, w)


def _q_body(xn_ref, w_ref, o_ref, *, scale):
    z = jnp.dot(xn_ref[...], w_ref[...], preferred_element_type=F32)
    o_ref[...] = (z * scale).astype(o_ref.dtype)


def _kv_body(xn_ref, w_ref, o32_ref, o16_ref):
    z = jnp.dot(xn_ref[...], w_ref[...], preferred_element_type=F32)
    o32_ref[...] = z
    o16_ref[...] = z.astype(BF16)


def _gate_body(xn_ref, w_ref, o_ref):
    z = jnp.dot(xn_ref[...], w_ref[...], preferred_element_type=F32)
    o_ref[...] = _silu(z).astype(o_ref.dtype)


def _log_sigmoid(z):
    return jnp.minimum(z, 0.0) - jnp.log1p(jnp.exp(-jnp.abs(z)))


def _forget_body(xn_ref, w_ref, b_ref, logf_ref, c_ref, carry_ref, *, tiles_per_seq):
    i = pl.program_id(0)
    tm = xn_ref.shape[0]
    logf = _log_sigmoid(jnp.dot(xn_ref[...], w_ref[...], preferred_element_type=F32) + b_ref[...])
    logf_ref[...] = logf

    @pl.when(i % tiles_per_seq == 0)
    def _():
        carry_ref[...] = jnp.zeros_like(carry_ref)

    row = lax.broadcasted_iota(jnp.int32, (tm, tm), 0)
    col = lax.broadcasted_iota(jnp.int32, (tm, tm), 1)
    c = _dot_01_by_f32(col <= row, logf) + carry_ref[0:1, :]
    c_ref[...] = c
    carry_ref[...] = jnp.broadcast_to(c[tm - 1:tm, :], carry_ref.shape)


def _forget_call(xn, wf_pad, bf_pad, tiles_per_seq, tm):
    m, d = xn.shape
    vmem = 2 * (tm * d * 2 + d * LANES * 2 + 2 * tm * LANES * 4) + tm * tm * 8
    return pl.pallas_call(
        functools.partial(_forget_body, tiles_per_seq=tiles_per_seq),
        grid=(m // tm,),
        in_specs=[pl.BlockSpec((tm, d), lambda i: (i, 0)),
                  pl.BlockSpec((d, LANES), lambda i: (0, 0)),
                  pl.BlockSpec((1, LANES), lambda i: (0, 0))],
        out_specs=[pl.BlockSpec((tm, LANES), lambda i: (i, 0)), pl.BlockSpec((tm, LANES), lambda i: (i, 0))],
        out_shape=[jax.ShapeDtypeStruct((m, LANES), F32), jax.ShapeDtypeStruct((m, LANES), F32)],
        scratch_shapes=[pltpu.VMEM((SUBLANES, LANES), F32)],
        compiler_params=_params(vmem, 1),
        name="forget_gate",
    )(xn, wf_pad, bf_pad)


def _conv_tail(zb, zg, u, u1, u2, wc_ref, m_ref):
    wc = wc_ref[...]
    y = wc[0:1, :] * u2 + wc[1:2, :] * u1 + wc[2:3, :] * u
    m_ref[...] = (zb * y * _silu(zg)).astype(m_ref.dtype)


def _conv_prompt_body(xn_ref, w_ref, wc_ref, m_ref, cn_ref, carry_ref, *, tiles_per_seq, tc):
    i = pl.program_id(1)
    tm = xn_ref.shape[0]
    z = jnp.dot(xn_ref[...], w_ref[...], preferred_element_type=F32)
    zb, zc, zh, zg = (z[:, k * tc:(k + 1) * tc] for k in range(4))
    u = zc * zh

    @pl.when(i % tiles_per_seq == 0)
    def _():
        carry_ref[...] = jnp.zeros_like(carry_ref)

    prev = carry_ref[...]
    p1 = prev[SUBLANES - 1:SUBLANES, :]
    p2 = prev[SUBLANES - 2:SUBLANES - 1, :]
    row = lax.broadcasted_iota(jnp.int32, (tm, tc), 0)
    u1 = jnp.where(row >= 1, pltpu.roll(u, 1, axis=0), p1)
    u2 = jnp.where(row >= 2, pltpu.roll(u, 2, axis=0), jnp.where(row == 1, p1, p2))
    carry_ref[...] = u[tm - SUBLANES:tm, :]
    cn_ref[0] = u[tm - (CONV_W - 1):tm, :]
    _conv_tail(zb, zg, u, u1, u2, wc_ref, m_ref)


def _conv_sample_body(xn_ref, w_ref, wc_ref, h1_ref, h2_ref, m_ref, u_ref, *, dec_seq, tc):
    tm = xn_ref.shape[0]
    z = jnp.dot(xn_ref[...], w_ref[...], preferred_element_type=F32)
    zb, zc, zh, zg = (z[:, k * tc:(k + 1) * tc] for k in range(4))
    u = zc * zh
    u_ref[...] = u
    t = lax.rem(lax.broadcasted_iota(jnp.int32, (tm, tc), 0), dec_seq)
    u1 = jnp.where(t >= 1, pltpu.roll(u, 1, axis=0), h1_ref[...])
    u2 = jnp.where(t >= 2, pltpu.roll(u, 2, axis=0), h2_ref[...])
    _conv_tail(zb, zg, u, u1, u2, wc_ref, m_ref)


def _conv_prompt_call(xn, w4, wconv, batch, tm, tc):
    m, d = xn.shape
    dc = wconv.shape[1]
    tiles_per_seq = m // batch // tm
    vmem = 2 * (tm * d * 2 + d * 4 * tc * 2 + tm * tc * 2) + tm * 4 * tc * 4 * 2
    return pl.pallas_call(
        functools.partial(_conv_prompt_body, tiles_per_seq=tiles_per_seq, tc=tc),
        grid=(dc // tc, m // tm),
        in_specs=[pl.BlockSpec((tm, d), lambda j, i: (i, 0)),
                  pl.BlockSpec((d, 4 * tc), lambda j, i: (0, j)),
                  pl.BlockSpec((CONV_W, tc), lambda j, i: (0, j))],
        out_specs=[pl.BlockSpec((tm, tc), lambda j, i: (i, j)),
                   pl.BlockSpec((1, CONV_W - 1, tc), lambda j, i: (i // tiles_per_seq, 0, j))],
        out_shape=[jax.ShapeDtypeStruct((m, dc), BF16), jax.ShapeDtypeStruct((batch, CONV_W - 1, dc), F32)],
        scratch_shapes=[pltpu.VMEM((SUBLANES, tc), F32)],
        compiler_params=_params(vmem, 2),
        name="conv_prompt",
    )(xn, w4, wconv)


def _conv_sample_call(xn, w4, wconv, h1, h2, dec_seq, tc):
    m, d = xn.shape
    dc = wconv.shape[1]
    vmem = 2 * (m * d * 2 + d * 4 * tc * 2 + m * tc * 14) + m * 4 * tc * 4 * 2
    return pl.pallas_call(
        functools.partial(_conv_sample_body, dec_seq=dec_seq, tc=tc),
        grid=(dc // tc,),
        in_specs=[pl.BlockSpec((m, d), lambda j: (0, 0)),
                  pl.BlockSpec((d, 4 * tc), lambda j: (0, j)),
                  pl.BlockSpec((CONV_W, tc), lambda j: (0, j)),
                  pl.BlockSpec((m, tc), lambda j: (0, j)),
                  pl.BlockSpec((m, tc), lambda j: (0, j))],
        out_specs=[pl.BlockSpec((m, tc), lambda j: (0, j)), pl.BlockSpec((m, tc), lambda j: (0, j))],
        out_shape=[jax.ShapeDtypeStruct((m, dc), BF16), jax.ShapeDtypeStruct((m, dc), F32)],
        compiler_params=_params(vmem, 1),
        name="conv_sample",
    )(xn, w4, wconv, h1, h2)


def _out_body(ma_ref, mc_ref, wa_ref, wc_ref, x_ref, g_ref, o_ref, *, tn, n_tiles):
    j = pl.program_id(1)
    z = (jnp.dot(ma_ref[...], wa_ref[...], preferred_element_type=F32)
         + jnp.dot(mc_ref[...], wc_ref[...], preferred_element_type=F32))
    o_ref[:, pl.ds(pl.multiple_of(j * tn, tn), tn)] = z

    @pl.when(j == n_tiles - 1)
    def _():
        zz = o_ref[...]
        ms = jnp.mean(zz * zz, axis=-1, keepdims=True)
        o_ref[...] = x_ref[...] + zz * lax.rsqrt(ms + EPS) * g_ref[...]


def _out_call(ma, mc, wo, x2d, g_row, tm, tn):
    m, da = ma.shape
    dcv = mc.shape[1]
    d = x2d.shape[1]
    assert da == dcv, "w_out row blocks are indexed in units of the attention width"
    n_tiles = d // tn
    vmem = 2 * (2 * tm * d * 4 + tm * (da + dcv) * 2 + (da + dcv) * tn * 2) + tm * tn * 4
    return pl.pallas_call(
        functools.partial(_out_body, tn=tn, n_tiles=n_tiles),
        grid=(m // tm, n_tiles),
        in_specs=[pl.BlockSpec((tm, da), lambda i, j: (i, 0)),
                  pl.BlockSpec((tm, dcv), lambda i, j: (i, 0)),
                  pl.BlockSpec((da, tn), lambda i, j: (0, j)),
                  pl.BlockSpec((dcv, tn), lambda i, j: (1, j)),
                  pl.BlockSpec((tm, d), lambda i, j: (i, 0)),
                  pl.BlockSpec((1, d), lambda i, j: (0, 0))],
        out_specs=pl.BlockSpec((tm, d), lambda i, j: (i, 0)),
        out_shape=jax.ShapeDtypeStruct((m, d), F32),
        compiler_params=_params(vmem, 2),
        name="out_proj",
    )(ma, mc, wo, wo, x2d, g_row)


def _attn_prompt_body(q_ref, k_ref, v_ref, c_ref, sg_ref, o_ref, vt_ref, crep_ref, *, blk, qsub):
    qi = pl.program_id(2)
    seq = k_ref.shape[1]

    @pl.when(qi == 0)
    def _():
        for c0 in range(0, seq, blk):
            vt_ref[:, c0:c0 + blk] = v_ref[0, c0:c0 + blk, :].astype(F32).T.astype(BF16)
            row = c_ref[0, 0, :, c0:c0 + blk] * LOG2E
            crep_ref[c0:c0 + blk, :] = jnp.broadcast_to(row, (LANES, blk)).T

    n_sub = blk // qsub
    q_subs = [q_ref[0, j * qsub:(j + 1) * qsub, :] for j in range(n_sub)]
    c_q = c_ref[0, 0, :, pl.ds(pl.multiple_of(qi * blk, blk), blk)] * LOG2E
    cq_subs = [c_q[:, j * qsub:(j + 1) * qsub] for j in range(n_sub)]

    def tile(kj, carry, diagonal):
        ks = pl.multiple_of(kj * blk, blk)
        k = k_ref[0, pl.ds(ks, blk), :]
        vt = vt_ref[:, pl.ds(ks, blk)]
        c_k = jnp.tile(crep_ref[pl.ds(ks, blk), :], (1, qsub // LANES))
        out = []
        for j, (m, l, acc) in enumerate(carry):
            st = lax.dot_general(k, q_subs[j], (((1,), (1,)), ((), ())), preferred_element_type=F32) - c_k
            if diagonal:
                key = lax.broadcasted_iota(jnp.int32, (blk, qsub), 0)
                qry = lax.broadcasted_iota(jnp.int32, (blk, qsub), 1) + j * qsub
                st = jnp.where(key <= qry, st, NEG_INF)
            m_new = jnp.maximum(m, jnp.max(st, axis=0, keepdims=True) + cq_subs[j])
            alpha = jnp.exp2(m - m_new)
            pt = jnp.exp2(st - (m_new - cq_subs[j]))
            l = alpha * l + jnp.sum(pt, axis=0, keepdims=True)
            acc = alpha * acc + jnp.dot(vt, pt.astype(BF16), preferred_element_type=F32)
            out.append((m_new, l, acc))
        return tuple(out)

    init = tuple((jnp.full((1, qsub), NEG_INF, F32), jnp.zeros((1, qsub), F32), jnp.zeros((HEAD_DIM, qsub), F32))
                 for _ in range(n_sub))
    carry = lax.fori_loop(0, qi // 2, lambda p, c: tile(2 * p + 1, tile(2 * p, c, False), False), init)
    carry = lax.cond(qi % 2 == 1, lambda c: tile(qi - 1, c, False), lambda c: c, carry)
    for j, (_, l, acc) in enumerate(tile(qi, carry, True)):
        o_ref[0, j * qsub:(j + 1) * qsub, :] = (
            (acc * (1.0 / l)).T * sg_ref[0, j * qsub:(j + 1) * qsub, :].astype(F32)).astype(o_ref.dtype)


def _attn_prompt_call(q, k, v, c_row, sg, blk):
    b, s, da = q.shape
    h = da // HEAD_DIM
    vmem = (2 * (2 * s * HEAD_DIM * 2 + 3 * blk * HEAD_DIM * 2 + s * 4 * SUBLANES)
            + s * HEAD_DIM * 2 + s * LANES * 4 + 6 * blk * blk * 4)
    return pl.pallas_call(
        functools.partial(_attn_prompt_body, blk=blk, qsub=blk),
        grid=(b, h, s // blk),
        in_specs=[pl.BlockSpec((1, blk, HEAD_DIM), lambda bi, hi, qi: (bi, qi, hi)),
                  pl.BlockSpec((1, s, HEAD_DIM), lambda bi, hi, qi: (bi, 0, hi)),
                  pl.BlockSpec((1, s, HEAD_DIM), lambda bi, hi, qi: (bi, 0, hi)),
                  pl.BlockSpec((1, 1, 1, s), lambda bi, hi, qi: (bi, hi, 0, 0)),
                  pl.BlockSpec((1, blk, HEAD_DIM), lambda bi, hi, qi: (bi, qi, hi))],
        out_specs=pl.BlockSpec((1, blk, HEAD_DIM), lambda bi, hi, qi: (bi, qi, hi)),
        out_shape=jax.ShapeDtypeStruct((b, s, da), BF16),
        scratch_shapes=[pltpu.VMEM((HEAD_DIM, s), BF16), pltpu.VMEM((s, LANES), F32)],
        compiler_params=_params(vmem, 3),
        name="attn_prompt",
    )(q, k, v, c_row, sg)


def _cum_body(pt_ref, *refs, n_pages, heads):
    del pt_ref
    x_refs, c_ref = refs[:n_pages + 1], refs[n_pages + 1]
    rows = x_refs[0].shape[-2]
    n_valid = (n_pages + 1) * rows
    n = -(-n_valid // LANES) * LANES
    blocks = [r[...] for r in x_refs]
    if n > n_valid:
        blocks.append(jnp.zeros((n - n_valid, LANES), F32))
    x = jnp.concatenate(blocks, axis=0)
    l1 = lax.broadcasted_iota(jnp.int32, (LANES, LANES), 0)
    l2 = lax.broadcasted_iota(jnp.int32, (LANES, LANES), 1)
    same_head = (l1 % heads) == (l2 % heads)
    within = same_head & ((l1 // heads) <= (l2 // heads))
    r1 = lax.broadcasted_iota(jnp.int32, (n, n), 0)
    r2 = lax.broadcasted_iota(jnp.int32, (n, n), 1)
    c = _dot_f32_by_01(x, within) + _dot_01_by_f32(r2 < r1, _dot_f32_by_01(x, same_head))
    c_ref[0] = c[:n_valid].reshape(n_pages + 1, rows, LANES)


def _cum_call(page_table, logf_pages, layer, new_rows, heads):
    db, n_pages = page_table.shape
    rows = logf_pages.shape[2]

    def page_map(p):
        return lambda s, pt: (layer, pt[s, p], 0, 0)

    in_specs = [pl.BlockSpec((None, None, rows, LANES), page_map(p)) for p in range(n_pages)]
    in_specs.append(pl.BlockSpec((None, rows, LANES), lambda s, pt: (s, 0, 0)))
    n = (n_pages + 1) * rows
    return pl.pallas_call(
        functools.partial(_cum_body, n_pages=n_pages, heads=heads),
        grid_spec=pltpu.PrefetchScalarGridSpec(
            num_scalar_prefetch=1, grid=(db,), in_specs=in_specs,
            out_specs=pl.BlockSpec((1, n_pages + 1, rows, LANES), lambda s, pt: (s, 0, 0, 0))),
        out_shape=jax.ShapeDtypeStruct((db, n_pages + 1, rows, LANES), F32),
        compiler_params=_params(4 * n * LANES * 4 + n * n * 8, 1),
        name="cache_logf_cumsum",
    )(page_table, *([logf_pages] * n_pages), new_rows)


def _attn_sample_body(pt_ref, *refs, group, heads, dec_seq, n_groups):
    del pt_ref
    k_refs, v_refs = refs[:group], refs[group:2 * group]
    q_ref, c_ref, cn_ref, cq_ref, kn_ref, vn_ref, sg_ref, o_ref, m_ref, l_ref, acc_ref = refs[2 * group:]
    g = pl.program_id(1)
    rows = heads * dec_seq
    q = q_ref[0]
    cq = cq_ref[0] * LOG2E

    @pl.when(g == 0)
    def _():
        m_ref[...] = jnp.full(m_ref.shape, NEG_INF, F32)
        l_ref[...] = jnp.zeros_like(l_ref)
        acc_ref[...] = jnp.zeros_like(acc_ref)

    r_head = lax.broadcasted_iota(jnp.int32, (rows, LANES), 0) % heads
    lane = lax.broadcasted_iota(jnp.int32, (rows, LANES), 1)
    same_head = (lane % heads) == r_head

    def local(kf, vf, bias_tiles):
        s = lax.dot_general(q, kf, (((1,), (1,)), ((), ())), preferred_element_type=F32)
        s = jnp.concatenate([s[:, t * LANES:(t + 1) * LANES] + bias_tiles[t] for t in range(len(bias_tiles))], axis=1)
        m_blk = jnp.max(s, axis=-1, keepdims=True)
        p = jnp.exp2(s - m_blk)
        return m_blk, jnp.sum(p, axis=-1, keepdims=True), jnp.dot(p.astype(BF16), vf, preferred_element_type=F32)

    def merge(parts):
        m, l, acc = m_ref[...], l_ref[...], acc_ref[...]
        for m_blk, l_blk, o_blk in parts:
            m_new = jnp.maximum(m, m_blk + cq)
            a_old, a_blk = jnp.exp2(m - m_new), jnp.exp2(m_blk + cq - m_new)
            l = a_old * l + a_blk * l_blk
            acc = a_old * acc + a_blk * o_blk
            m = m_new
        m_ref[...], l_ref[...], acc_ref[...] = m, l, acc

    head_mask = jnp.where(same_head, 0.0, NEG_INF).astype(F32)
    tiles_per_page = k_refs[0].shape[0] * heads // LANES
    parts = []
    for i in range(group):
        kf = k_refs[i][...].reshape(-1, HEAD_DIM).astype(BF16)
        vf = v_refs[i][...].reshape(-1, HEAD_DIM).astype(BF16)
        c = c_ref[0, i] * LOG2E
        parts.append(local(kf, vf, [head_mask - c[t:t + 1, :] for t in range(tiles_per_page)]))
    merge(parts)

    @pl.when(g == n_groups - 1)
    def _():
        r_query = lax.broadcasted_iota(jnp.int32, (rows, LANES), 0) // heads
        ok = same_head & ((lane // heads) <= r_query) & (lane < rows)
        bias = jnp.where(ok, 0.0, NEG_INF).astype(F32) - cn_ref[0, 0][0:1, :] * LOG2E
        merge([local(kn_ref[0], vn_ref[0], [bias])])
        o_ref[0] = acc_ref[...] * (1.0 / l_ref[...]) * sg_ref[0].astype(F32)


def _attn_sample_call(page_table, cache_k, cache_v, layer, q_rows, c_all, cq_col, kn_rows, vn_rows, sg_rows,
                      group):
    db, n_pages = page_table.shape
    _, _, page, heads, hd = cache_k.shape
    rows = q_rows.shape[1]
    dec_seq = rows // heads
    n_groups = n_pages // group
    tiles_per_page = page * heads // LANES

    def page_map(i):
        return lambda s, g, pt: (layer, pt[s, g * group + i], 0, 0, 0)

    kv_spec = [pl.BlockSpec((None, None, page, heads, hd), page_map(i)) for i in range(group)]
    in_specs = kv_spec + kv_spec + [
        pl.BlockSpec((1, rows, hd), lambda s, g, pt: (s, 0, 0)),
        pl.BlockSpec((1, group, tiles_per_page, LANES), lambda s, g, pt: (s, g, 0, 0)),
        pl.BlockSpec((1, 1, tiles_per_page, LANES), lambda s, g, pt: (s, n_pages, 0, 0)),
        pl.BlockSpec((1, rows, 1), lambda s, g, pt: (s, 0, 0)),
        pl.BlockSpec((1, LANES, hd), lambda s, g, pt: (s, 0, 0)),
        pl.BlockSpec((1, LANES, hd), lambda s, g, pt: (s, 0, 0)),
        pl.BlockSpec((1, rows, hd), lambda s, g, pt: (s, 0, 0)),
    ]
    vmem = 2 * 2 * group * page * heads * hd * 4 + 4 * rows * page * heads * 4 + (4 << 20)
    return pl.pallas_call(
        functools.partial(_attn_sample_body, group=group, heads=heads, dec_seq=dec_seq, n_groups=n_groups),
        grid_spec=pltpu.PrefetchScalarGridSpec(
            num_scalar_prefetch=1, grid=(db, n_groups), in_specs=in_specs,
            out_specs=pl.BlockSpec((1, rows, hd), lambda s, g, pt: (s, 0, 0)),
            scratch_shapes=[pltpu.VMEM((rows, 1), F32), pltpu.VMEM((rows, 1), F32), pltpu.VMEM((rows, hd), F32)]),
        out_shape=jax.ShapeDtypeStruct((db, rows, hd), F32),
        compiler_params=_params(vmem, 2),
        name="attn_sample",
    )(page_table, *([cache_k] * group), *([cache_v] * group), q_rows, c_all, c_all, cq_col, kn_rows, vn_rows,
      sg_rows)


def _pick(n, pref):
    t = min(pref, n)
    assert n % t == 0, (n, t)
    return t


def _layer(xp, xs, cache_k, cache_v, logf_pages, state_conv_l, page_table, layer,
           g_pre, w_in, b_f, w_conv, w_out, g_post):
    b, s, d = xp.shape
    db, ds, _ = xs.shape
    heads = b_f.shape[0]
    da = heads * HEAD_DIM
    dc = w_conv.shape[1]
    page = cache_k.shape[2]
    n_pages = page_table.shape[1]
    mp, msz = b * s, db * ds
    scale = HEAD_DIM ** -0.5 * LOG2E

    tc = _pick(dc, 256)
    wqkv, wf, wga, w4 = _weights_call(w_in, heads, da, dc, tc)
    bf = jnp.pad(b_f, (0, LANES - heads)).reshape(1, LANES).astype(F32)
    wo = w_out.astype(BF16)
    gpre = g_pre.reshape(1, d)
    gpost = g_post.reshape(1, d)

    def project(x2d, tm):
        xn = _rmsnorm_bf16(x2d, gpre)
        tn = _pick(da, 1024)
        (qv,) = _proj_call(functools.partial(_q_body, scale=scale), xn, wqkv, [BF16], tm, tn, "proj_q", da, 0)
        k32, k16 = _proj_call(_kv_body, xn, wqkv, [F32, BF16], tm, tn, "proj_k", da, da)
        v32, v16 = _proj_call(_kv_body, xn, wqkv, [F32, BF16], tm, tn, "proj_v", da, 2 * da)
        (sg,) = _proj_call(_gate_body, xn, wga, [BF16], tm, tn, "proj_gate")
        return xn, qv, k32, k16, v32, v16, sg

    xp2 = xp.reshape(mp, d)
    tm_p = _pick(s, 512)
    xn, qv, k32, k16, v32, v16, sg = project(xp2, tm_p)
    logf_pad, c_pad = _forget_call(xn, wf, bf, s // tm_p, tm_p)
    logf_p = logf_pad[:, :heads].reshape(b, s, heads)
    c_p = c_pad[:, :heads].reshape(b, s, heads)
    c_row = c_p.transpose(0, 2, 1).reshape(b, heads, 1, s)
    blk = _pick(s, 512)
    ma_p = _attn_prompt_call(qv.reshape(b, s, da), k16.reshape(b, s, da), v16.reshape(b, s, da),
                             c_row, sg.reshape(b, s, da), blk)
    mc_p, conv_p = _conv_prompt_call(xn, w4, w_conv, b, tm_p, tc)
    tm_o = _pick(s, 512)
    yp = _out_call(ma_p.reshape(mp, da), mc_p, wo, xp2, gpost, tm_o, _pick(d, 512)).reshape(b, s, d)
    k_p = k32.reshape(b, s, heads, HEAD_DIM)
    v_p = v32.reshape(b, s, heads, HEAD_DIM)

    xs2 = xs.reshape(msz, d)
    tm_s = _pick(msz, 512)
    xn, qv, k32, k16, v32, v16, sg = project(xs2, tm_s)
    logf_pad, _ = _forget_call(xn, wf, bf, 1, tm_s)
    logf_s = logf_pad[:, :heads].reshape(db, ds, heads)
    rows = ds * heads
    tiles_per_page = page * heads // LANES
    new_rows = jnp.pad(logf_s.reshape(db, 1, rows), ((0, 0), (0, tiles_per_page - 1), (0, LANES - rows)))
    c_all = _cum_call(page_table, logf_pages, layer, new_rows, heads)
    cq_s = c_all[:, n_pages, 0, :rows].reshape(db, rows, 1)
    pad_rows = ((0, 0), (0, LANES - rows), (0, 0))
    kn = jnp.pad(k16.reshape(db, rows, HEAD_DIM), pad_rows)
    vn = jnp.pad(v16.reshape(db, rows, HEAD_DIM), pad_rows)
    o_s = _attn_sample_call(page_table, cache_k, cache_v, layer, qv.reshape(db, rows, HEAD_DIM), c_all, cq_s,
                            kn, vn, sg.reshape(db, rows, HEAD_DIM), _pick(n_pages, 8))
    ma_s = o_s.reshape(msz, da).astype(BF16)
    hist = state_conv_l
    zeros = jnp.zeros((db, 1, dc), F32)
    h1 = jnp.concatenate([hist[:, 1:2]] + [zeros] * (ds - 1), axis=1).reshape(msz, dc)
    h2 = jnp.concatenate([hist[:, 0:1], hist[:, 1:2]] + [zeros] * (ds - 2), axis=1).reshape(msz, dc)
    mc_s, u_s = _conv_sample_call(xn, w4, w_conv, h1, h2, ds, tc)
    ys = _out_call(ma_s, mc_s, wo, xs2, gpost, _pick(msz, 512), _pick(d, 512)).reshape(db, ds, d)
    conv_s = u_s.reshape(db, ds, dc)[:, ds - (CONV_W - 1):]
    k_s = k32.reshape(db, ds, heads, HEAD_DIM)
    v_s = v32.reshape(db, ds, heads, HEAD_DIM)
    return yp, ys, (k_p, v_p, logf_p, conv_p, k_s, v_s, logf_s, conv_s)


def kernel(x_prompt, x_sample, cache_k, cache_v, cache_logf, state_conv, page_table, g_pre, w_in, b_f, w_conv,
           w_out, g_post):
    depth, n_pool, page, heads = cache_logf.shape
    assert cache_k.shape[-1] == HEAD_DIM and (page * heads) % LANES == 0 and LANES % heads == 0
    assert x_sample.shape[1] >= CONV_W - 1 and x_sample.shape[1] * heads <= LANES
    logf_pages = cache_logf.reshape(depth, n_pool, page * heads // LANES, LANES)
    xp, xs = x_prompt, x_sample
    per_layer = []
    for layer in range(depth):
        xp, xs, outs = _layer(xp, xs, cache_k, cache_v, logf_pages, state_conv[layer], page_table, layer,
                              g_pre[layer], w_in[layer], b_f[layer], w_conv[layer], w_out[layer], g_post[layer])
        per_layer.append(outs)
    stacked = [jnp.stack(leaf) for leaf in zip(*per_layer)]
    return (xp, xs, *stacked)
```

```python
import functools

import jax
import jax.numpy as jnp
from jax import lax
from jax.experimental import pallas as pl
from jax.experimental.pallas import tpu as pltpu

HEAD_DIM = 128
CONV_W = 3
EPS = 1e-6
LANES = 128
SUBLANES = 8
V7X_VMEM_BYTES = 64 * 1024 * 1024
VMEM_BUDGET_BYTES = 58 * 1024 * 1024
F32 = jnp.float32
BF16 = jnp.bfloat16
NEG_INF = float("-inf")
LOG2E = 1.4426950408889634
UNROLL_KEY_TILES = 2


def _params(vmem_bytes, n_grid):
    limit = min(int(vmem_bytes * 1.25) + (4 << 20), VMEM_BUDGET_BYTES)
    return pltpu.CompilerParams(dimension_semantics=("arbitrary",) * n_grid, vmem_limit_bytes=limit)


def _silu(z):
    return z * (1.0 / (1.0 + jnp.exp(-z)))


def _split3(x):
    hi = x.astype(BF16)
    r = x - hi.astype(F32)
    mid = r.astype(BF16)
    lo = (r - mid.astype(F32)).astype(BF16)
    return hi, mid, lo


def _dot_f32_by_01(x, w01):
    w = w01.astype(BF16)
    hi, mid, lo = (jnp.dot(p, w, preferred_element_type=F32) for p in _split3(x))
    return (lo + mid) + hi


def _dot_01_by_f32(w01, x):
    w = w01.astype(BF16)
    hi, mid, lo = (jnp.dot(w, p, preferred_element_type=F32) for p in _split3(x))
    return (lo + mid) + hi


def _rmsnorm_kernel(x_ref, g_ref, o_ref):
    x = x_ref[...]
    ms = jnp.mean(x * x, axis=-1, keepdims=True)
    o_ref[...] = (x * lax.rsqrt(ms + EPS) * g_ref[...]).astype(o_ref.dtype)


def _rmsnorm_bf16(x2d, g_row):
    m, d = x2d.shape
    tm = min(256, m)
    return pl.pallas_call(
        _rmsnorm_kernel,
        grid=(m // tm,),
        in_specs=[pl.BlockSpec((tm, d), lambda i: (i, 0)), pl.BlockSpec((1, d), lambda i: (0, 0))],
        out_specs=pl.BlockSpec((tm, d), lambda i: (i, 0)),
        out_shape=jax.ShapeDtypeStruct((m, d), BF16),
        compiler_params=_params(2 * tm * d * 6, 1),
        name="rmsnorm_pre",
    )(x2d, g_row)


def _weights_body(w_ref, o_ref):
    o_ref[...] = w_ref[0].T.astype(BF16)


def _weights_call(w_t, layer, row0, n_rows, out_block_of, name):
    d = w_t.shape[2]
    tr = _pick(n_rows, 256)
    assert row0 % SUBLANES == 0
    return pl.pallas_call(
        _weights_body,
        grid=(n_rows // tr,),
        in_specs=[pl.BlockSpec((pl.Element(1), pl.Element(tr), pl.Element(d)), lambda i: (layer, pl.multiple_of(row0 + i * tr, SUBLANES), 0))],
        out_specs=pl.BlockSpec((d, tr), lambda i: (0, out_block_of(i))),
        out_shape=jax.ShapeDtypeStruct((d, n_rows), BF16),
        compiler_params=_params(2 * tr * d * 6 + tr * d * 8, 1),
        name=name,
    )(w_t)


def _proj_call(body, xn, w, out_dtypes, tm, tn, name, n=None, col0=0):
    m, d = xn.shape
    n = w.shape[1] if n is None else n
    assert col0 % tn == 0 and n % tn == 0
    jb = col0 // tn
    out_bytes = sum(jnp.dtype(t).itemsize for t in out_dtypes)
    vmem = 2 * (tm * d * 2 + d * tn * 2 + tm * tn * out_bytes) + tm * tn * 4
    return pl.pallas_call(
        body,
        grid=(n // tn, m // tm),
        in_specs=[pl.BlockSpec((tm, d), lambda j, i: (i, 0)), pl.BlockSpec((d, tn), lambda j, i: (0, j + jb))],
        out_specs=[pl.BlockSpec((tm, tn), lambda j, i: (i, j)) for _ in out_dtypes],
        out_shape=[jax.ShapeDtypeStruct((m, n), t) for t in out_dtypes],
        compiler_params=_params(vmem, 2),
        name=name,
    )(xn, w)


def _q_body(xn_ref, w_ref, o_ref, *, scale):
    z = jnp.dot(xn_ref[...], w_ref[...], preferred_element_type=F32)
    o_ref[...] = (z * scale).astype(o_ref.dtype)


def _kv_body(xn_ref, w_ref, o32_ref, o16_ref):
    z = jnp.dot(xn_ref[...], w_ref[...], preferred_element_type=F32)
    o32_ref[...] = z
    o16_ref[...] = z.astype(BF16)


def _gate_body(xn_ref, w_ref, o_ref):
    z = jnp.dot(xn_ref[...], w_ref[...], preferred_element_type=F32)
    o_ref[...] = _silu(z).astype(o_ref.dtype)


def _log_sigmoid(z):
    return jnp.minimum(z, 0.0) - jnp.log1p(jnp.exp(-jnp.abs(z)))


def _forget_body(xn_ref, w_ref, b_ref, logf_ref, c_ref, carry_ref, *, tiles_per_seq):
    i = pl.program_id(0)
    tm = xn_ref.shape[0]
    z = lax.dot_general(xn_ref[...], w_ref[...], (((1,), (1,)), ((), ())), preferred_element_type=F32)
    logf = _log_sigmoid(z + b_ref[...])
    logf_ref[...] = logf

    @pl.when(i % tiles_per_seq == 0)
    def _():
        carry_ref[...] = jnp.zeros_like(carry_ref)

    row = lax.broadcasted_iota(jnp.int32, (tm, tm), 0)
    col = lax.broadcasted_iota(jnp.int32, (tm, tm), 1)
    c = _dot_01_by_f32(col <= row, logf) + carry_ref[0:1, :]
    c_ref[...] = c
    carry_ref[...] = jnp.broadcast_to(c[tm - 1:tm, :], carry_ref.shape)


def _forget_call(xn, wf_pad, bf_pad, tiles_per_seq, tm):
    m, d = xn.shape
    vmem = 2 * (tm * d * 2 + d * LANES * 2 + 2 * tm * LANES * 4) + tm * tm * 8
    return pl.pallas_call(
        functools.partial(_forget_body, tiles_per_seq=tiles_per_seq),
        grid=(m // tm,),
        in_specs=[pl.BlockSpec((tm, d), lambda i: (i, 0)),
                  pl.BlockSpec((LANES, d), lambda i: (0, 0)),
                  pl.BlockSpec((1, LANES), lambda i: (0, 0))],
        out_specs=[pl.BlockSpec((tm, LANES), lambda i: (i, 0)), pl.BlockSpec((tm, LANES), lambda i: (i, 0))],
        out_shape=[jax.ShapeDtypeStruct((m, LANES), F32), jax.ShapeDtypeStruct((m, LANES), F32)],
        scratch_shapes=[pltpu.VMEM((SUBLANES, LANES), F32)],
        compiler_params=_params(vmem, 1),
        name="forget_gate",
    )(xn, wf_pad, bf_pad)


def _conv_tail(zb, zg, u, u1, u2, wc_ref, m_ref):
    wc = wc_ref[...]
    y = wc[0:1, :] * u2 + wc[1:2, :] * u1 + wc[2:3, :] * u
    m_ref[...] = (zb * y * _silu(zg)).astype(m_ref.dtype)


def _conv_prompt_body(xn_ref, w_ref, wc_ref, m_ref, cn_ref, carry_ref, *, tiles_per_seq, tc):
    i = pl.program_id(1)
    tm = xn_ref.shape[0]
    z = jnp.dot(xn_ref[...], w_ref[...], preferred_element_type=F32)
    zb, zc, zh, zg = (z[:, k * tc:(k + 1) * tc] for k in range(4))
    u = zc * zh

    @pl.when(i % tiles_per_seq == 0)
    def _():
        carry_ref[...] = jnp.zeros_like(carry_ref)

    prev = carry_ref[...]
    p1 = prev[SUBLANES - 1:SUBLANES, :]
    p2 = prev[SUBLANES - 2:SUBLANES - 1, :]
    row = lax.broadcasted_iota(jnp.int32, (tm, tc), 0)
    u1 = jnp.where(row >= 1, pltpu.roll(u, 1, axis=0), p1)
    u2 = jnp.where(row >= 2, pltpu.roll(u, 2, axis=0), jnp.where(row == 1, p1, p2))
    carry_ref[...] = u[tm - SUBLANES:tm, :]
    cn_ref[0] = u[tm - (CONV_W - 1):tm, :]
    _conv_tail(zb, zg, u, u1, u2, wc_ref, m_ref)


def _conv_sample_body(xn_ref, w_ref, wc_ref, h1_ref, h2_ref, m_ref, u_ref, *, dec_seq, tc):
    tm = xn_ref.shape[0]
    z = jnp.dot(xn_ref[...], w_ref[...], preferred_element_type=F32)
    zb, zc, zh, zg = (z[:, k * tc:(k + 1) * tc] for k in range(4))
    u = zc * zh
    u_ref[...] = u
    t = lax.rem(lax.broadcasted_iota(jnp.int32, (tm, tc), 0), dec_seq)
    u1 = jnp.where(t >= 1, pltpu.roll(u, 1, axis=0), h1_ref[...])
    u2 = jnp.where(t >= 2, pltpu.roll(u, 2, axis=0), h2_ref[...])
    _conv_tail(zb, zg, u, u1, u2, wc_ref, m_ref)


def _conv_prompt_call(xn, w4, wconv, batch, tm, tc):
    m, d = xn.shape
    dc = wconv.shape[1]
    tiles_per_seq = m // batch // tm
    vmem = 2 * (tm * d * 2 + d * 4 * tc * 2 + tm * tc * 2) + tm * 4 * tc * 4 * 2
    return pl.pallas_call(
        functools.partial(_conv_prompt_body, tiles_per_seq=tiles_per_seq, tc=tc),
        grid=(dc // tc, m // tm),
        in_specs=[pl.BlockSpec((tm, d), lambda j, i: (i, 0)),
                  pl.BlockSpec((d, 4 * tc), lambda j, i: (0, j)),
                  pl.BlockSpec((CONV_W, tc), lambda j, i: (0, j))],
        out_specs=[pl.BlockSpec((tm, tc), lambda j, i: (i, j)),
                   pl.BlockSpec((1, CONV_W - 1, tc), lambda j, i: (i // tiles_per_seq, 0, j))],
        out_shape=[jax.ShapeDtypeStruct((m, dc), BF16), jax.ShapeDtypeStruct((batch, CONV_W - 1, dc), F32)],
        scratch_shapes=[pltpu.VMEM((SUBLANES, tc), F32)],
        compiler_params=_params(vmem, 2),
        name="conv_prompt",
    )(xn, w4, wconv)


def _conv_sample_call(xn, w4, wconv, h1, h2, dec_seq, tc):
    m, d = xn.shape
    dc = wconv.shape[1]
    vmem = 2 * (m * d * 2 + d * 4 * tc * 2 + m * tc * 14) + m * 4 * tc * 4 * 2
    return pl.pallas_call(
        functools.partial(_conv_sample_body, dec_seq=dec_seq, tc=tc),
        grid=(dc // tc,),
        in_specs=[pl.BlockSpec((m, d), lambda j: (0, 0)),
                  pl.BlockSpec((d, 4 * tc), lambda j: (0, j)),
                  pl.BlockSpec((CONV_W, tc), lambda j: (0, j)),
                  pl.BlockSpec((m, tc), lambda j: (0, j)),
                  pl.BlockSpec((m, tc), lambda j: (0, j))],
        out_specs=[pl.BlockSpec((m, tc), lambda j: (0, j)), pl.BlockSpec((m, tc), lambda j: (0, j))],
        out_shape=[jax.ShapeDtypeStruct((m, dc), BF16), jax.ShapeDtypeStruct((m, dc), F32)],
        compiler_params=_params(vmem, 1),
        name="conv_sample",
    )(xn, w4, wconv, h1, h2)


def _out_body(ma_ref, mc_ref, wa_ref, wc_ref, x_ref, g_ref, o_ref, *, tn, n_tiles):
    j = pl.program_id(1)
    z = (jnp.dot(ma_ref[...], wa_ref[...], preferred_element_type=F32)
         + jnp.dot(mc_ref[...], wc_ref[...], preferred_element_type=F32))
    o_ref[:, pl.ds(pl.multiple_of(j * tn, tn), tn)] = z

    @pl.when(j == n_tiles - 1)
    def _():
        zz = o_ref[...]
        ms = jnp.mean(zz * zz, axis=-1, keepdims=True)
        o_ref[...] = x_ref[...] + zz * lax.rsqrt(ms + EPS) * g_ref[...]


def _out_call(ma, mc, wo, x2d, g_row, tm, tn):
    m, da = ma.shape
    dcv = mc.shape[1]
    d = x2d.shape[1]
    assert da == dcv, "w_out row blocks are indexed in units of the attention width"
    n_tiles = d // tn
    vmem = 2 * (2 * tm * d * 4 + tm * (da + dcv) * 2 + (da + dcv) * tn * 2) + tm * tn * 4
    return pl.pallas_call(
        functools.partial(_out_body, tn=tn, n_tiles=n_tiles),
        grid=(m // tm, n_tiles),
        in_specs=[pl.BlockSpec((tm, da), lambda i, j: (i, 0)),
                  pl.BlockSpec((tm, dcv), lambda i, j: (i, 0)),
                  pl.BlockSpec((da, tn), lambda i, j: (0, j)),
                  pl.BlockSpec((dcv, tn), lambda i, j: (1, j)),
                  pl.BlockSpec((tm, d), lambda i, j: (i, 0)),
                  pl.BlockSpec((1, d), lambda i, j: (0, 0))],
        out_specs=pl.BlockSpec((tm, d), lambda i, j: (i, 0)),
        out_shape=jax.ShapeDtypeStruct((m, d), F32),
        compiler_params=_params(vmem, 2),
        name="out_proj",
    )(ma, mc, wo, wo, x2d, g_row)


def _attn_prompt_body(q_ref, k_ref, v_ref, c_ref, sg_ref, o_ref, vt_ref, crep_ref, *, blk, qsub):
    qi = pl.program_id(2)
    seq = k_ref.shape[1]

    @pl.when(qi == 0)
    def _():
        for c0 in range(0, seq, blk):
            vt_ref[:, c0:c0 + blk] = v_ref[0, c0:c0 + blk, :].astype(F32).T.astype(BF16)
            row = c_ref[0, 0, :, c0:c0 + blk] * LOG2E
            crep_ref[c0:c0 + blk, :] = jnp.broadcast_to(row, (LANES, blk)).T

    n_sub = blk // qsub
    q_subs = [q_ref[0, j * qsub:(j + 1) * qsub, :] for j in range(n_sub)]
    c_q = c_ref[0, 0, :, pl.ds(pl.multiple_of(qi * blk, blk), blk)] * LOG2E
    cq_subs = [c_q[:, j * qsub:(j + 1) * qsub] for j in range(n_sub)]

    def tile(kj, carry, diagonal):
        ks = pl.multiple_of(kj * blk, blk)
        k = k_ref[0, pl.ds(ks, blk), :]
        vt = vt_ref[:, pl.ds(ks, blk)]
        c_k = jnp.tile(crep_ref[pl.ds(ks, blk), :], (1, qsub // LANES))
        out = []
        for j, (m, l, acc) in enumerate(carry):
            st = lax.dot_general(k, q_subs[j], (((1,), (1,)), ((), ())), preferred_element_type=F32) - c_k
            if diagonal:
                key = lax.broadcasted_iota(jnp.int32, (blk, qsub), 0)
                qry = lax.broadcasted_iota(jnp.int32, (blk, qsub), 1) + j * qsub
                st = jnp.where(key <= qry, st, NEG_INF)
            m_new = jnp.maximum(m, jnp.max(st, axis=0, keepdims=True) + cq_subs[j])
            alpha = jnp.exp2(m - m_new)
            pt = jnp.exp2(st - (m_new - cq_subs[j]))
            l = alpha * l + jnp.sum(pt, axis=0, keepdims=True)
            acc = alpha * acc + jnp.dot(vt, pt.astype(BF16), preferred_element_type=F32)
            out.append((m_new, l, acc))
        return tuple(out)

    init = tuple((jnp.full((1, qsub), NEG_INF, F32), jnp.zeros((1, qsub), F32), jnp.zeros((HEAD_DIM, qsub), F32))
                 for _ in range(n_sub))
    def multi(p, c):
        for u in range(UNROLL_KEY_TILES):
            c = tile(UNROLL_KEY_TILES * p + u, c, False)
        return c

    n_multi = qi // UNROLL_KEY_TILES
    carry = lax.fori_loop(0, n_multi, multi, init)
    carry = lax.fori_loop(n_multi * UNROLL_KEY_TILES, qi, lambda kj, c: tile(kj, c, False), carry)
    for j, (_, l, acc) in enumerate(tile(qi, carry, True)):
        o_ref[0, j * qsub:(j + 1) * qsub, :] = (
            (acc * (1.0 / l)).T * sg_ref[0, j * qsub:(j + 1) * qsub, :].astype(F32)).astype(o_ref.dtype)


def _attn_prompt_call(q, k, v, c_row, sg, blk):
    b, s, da = q.shape
    h = da // HEAD_DIM
    vmem = (2 * (2 * s * HEAD_DIM * 2 + 3 * blk * HEAD_DIM * 2 + s * 4 * SUBLANES)
            + s * HEAD_DIM * 2 + s * LANES * 4 + 6 * blk * blk * 4)
    return pl.pallas_call(
        functools.partial(_attn_prompt_body, blk=blk, qsub=blk),
        grid=(b, h, s // blk),
        in_specs=[pl.BlockSpec((1, blk, HEAD_DIM), lambda bi, hi, qi: (bi, qi, hi)),
                  pl.BlockSpec((1, s, HEAD_DIM), lambda bi, hi, qi: (bi, 0, hi)),
                  pl.BlockSpec((1, s, HEAD_DIM), lambda bi, hi, qi: (bi, 0, hi)),
                  pl.BlockSpec((1, 1, 1, s), lambda bi, hi, qi: (bi, hi, 0, 0)),
                  pl.BlockSpec((1, blk, HEAD_DIM), lambda bi, hi, qi: (bi, qi, hi))],
        out_specs=pl.BlockSpec((1, blk, HEAD_DIM), lambda bi, hi, qi: (bi, qi, hi)),
        out_shape=jax.ShapeDtypeStruct((b, s, da), BF16),
        scratch_shapes=[pltpu.VMEM((HEAD_DIM, s), BF16), pltpu.VMEM((s, LANES), F32)],
        compiler_params=_params(vmem, 3),
        name="attn_prompt",
    )(q, k, v, c_row, sg)


def _cum_body(pt_ref, *refs, n_pages, heads):
    del pt_ref
    x_refs, c_ref = refs[:n_pages + 1], refs[n_pages + 1]
    rows = x_refs[0].shape[-2]
    n_valid = (n_pages + 1) * rows
    n = -(-n_valid // LANES) * LANES
    blocks = [r[...] for r in x_refs]
    if n > n_valid:
        blocks.append(jnp.zeros((n - n_valid, LANES), F32))
    x = jnp.concatenate(blocks, axis=0)
    l1 = lax.broadcasted_iota(jnp.int32, (LANES, LANES), 0)
    l2 = lax.broadcasted_iota(jnp.int32, (LANES, LANES), 1)
    same_head = (l1 % heads) == (l2 % heads)
    within = same_head & ((l1 // heads) <= (l2 // heads))
    r1 = lax.broadcasted_iota(jnp.int32, (n, n), 0)
    r2 = lax.broadcasted_iota(jnp.int32, (n, n), 1)
    c = _dot_f32_by_01(x, within) + _dot_01_by_f32(r2 < r1, _dot_f32_by_01(x, same_head))
    c_ref[0] = c[:n_valid].reshape(n_pages + 1, rows, LANES)


def _cum_call(page_table, logf_pages, layer, new_rows, heads):
    db, n_pages = page_table.shape
    rows = logf_pages.shape[2]

    def page_map(p):
        return lambda s, pt: (layer, pt[s, p], 0, 0)

    in_specs = [pl.BlockSpec((None, None, rows, LANES), page_map(p)) for p in range(n_pages)]
    in_specs.append(pl.BlockSpec((None, rows, LANES), lambda s, pt: (s, 0, 0)))
    n = (n_pages + 1) * rows
    return pl.pallas_call(
        functools.partial(_cum_body, n_pages=n_pages, heads=heads),
        grid_spec=pltpu.PrefetchScalarGridSpec(
            num_scalar_prefetch=1, grid=(db,), in_specs=in_specs,
            out_specs=pl.BlockSpec((1, n_pages + 1, rows, LANES), lambda s, pt: (s, 0, 0, 0))),
        out_shape=jax.ShapeDtypeStruct((db, n_pages + 1, rows, LANES), F32),
        compiler_params=_params(4 * n * LANES * 4 + n * n * 8, 1),
        name="cache_logf_cumsum",
    )(page_table, *([logf_pages] * n_pages), new_rows)


def _attn_sample_body(pt_ref, *refs, group, heads, dec_seq, n_groups):
    del pt_ref
    k_refs, v_refs = refs[:group], refs[group:2 * group]
    q_ref, c_ref, cn_ref, cq_ref, kn_ref, vn_ref, sg_ref, o_ref, m_ref, l_ref, acc_ref = refs[2 * group:]
    g = pl.program_id(1)
    rows = heads * dec_seq
    q = q_ref[0]
    cq = cq_ref[0] * LOG2E

    @pl.when(g == 0)
    def _():
        m_ref[...] = jnp.full(m_ref.shape, NEG_INF, F32)
        l_ref[...] = jnp.zeros_like(l_ref)
        acc_ref[...] = jnp.zeros_like(acc_ref)

    r_head = lax.broadcasted_iota(jnp.int32, (rows, LANES), 0) % heads
    lane = lax.broadcasted_iota(jnp.int32, (rows, LANES), 1)
    same_head = (lane % heads) == r_head

    def local(kf, vf, bias_tiles):
        s = lax.dot_general(q, kf, (((1,), (1,)), ((), ())), preferred_element_type=F32)
        s = jnp.concatenate([s[:, t * LANES:(t + 1) * LANES] + bias_tiles[t] for t in range(len(bias_tiles))], axis=1)
        m_blk = jnp.max(s, axis=-1, keepdims=True)
        p = jnp.exp2(s - m_blk)
        return m_blk, jnp.sum(p, axis=-1, keepdims=True), jnp.dot(p.astype(BF16), vf, preferred_element_type=F32)

    def merge(parts):
        m, l, acc = m_ref[...], l_ref[...], acc_ref[...]
        for m_blk, l_blk, o_blk in parts:
            m_new = jnp.maximum(m, m_blk + cq)
            a_old, a_blk = jnp.exp2(m - m_new), jnp.exp2(m_blk + cq - m_new)
            l = a_old * l + a_blk * l_blk
            acc = a_old * acc + a_blk * o_blk
            m = m_new
        m_ref[...], l_ref[...], acc_ref[...] = m, l, acc

    head_mask = jnp.where(same_head, 0.0, NEG_INF).astype(F32)
    tiles_per_page = k_refs[0].shape[0] * heads // LANES
    parts = []
    for i in range(group):
        kf = k_refs[i][...].reshape(-1, HEAD_DIM).astype(BF16)
        vf = v_refs[i][...].reshape(-1, HEAD_DIM).astype(BF16)
        c = c_ref[0, i] * LOG2E
        parts.append(local(kf, vf, [head_mask - c[t:t + 1, :] for t in range(tiles_per_page)]))
    merge(parts)

    @pl.when(g == n_groups - 1)
    def _():
        r_query = lax.broadcasted_iota(jnp.int32, (rows, LANES), 0) // heads
        ok = same_head & ((lane // heads) <= r_query) & (lane < rows)
        bias = jnp.where(ok, 0.0, NEG_INF).astype(F32) - cn_ref[0, 0][0:1, :] * LOG2E
        merge([local(kn_ref[0], vn_ref[0], [bias])])
        o_ref[0] = acc_ref[...] * (1.0 / l_ref[...]) * sg_ref[0].astype(F32)


def _attn_sample_call(page_table, cache_k, cache_v, layer, q_rows, c_all, cq_col, kn_rows, vn_rows, sg_rows,
                      group):
    db, n_pages = page_table.shape
    _, _, page, heads, hd = cache_k.shape
    rows = q_rows.shape[1]
    dec_seq = rows // heads
    n_groups = n_pages // group
    tiles_per_page = page * heads // LANES

    def page_map(i):
        return lambda s, g, pt: (layer, pt[s, g * group + i], 0, 0, 0)

    kv_spec = [pl.BlockSpec((None, None, page, heads, hd), page_map(i)) for i in range(group)]
    in_specs = kv_spec + kv_spec + [
        pl.BlockSpec((1, rows, hd), lambda s, g, pt: (s, 0, 0)),
        pl.BlockSpec((1, group, tiles_per_page, LANES), lambda s, g, pt: (s, g, 0, 0)),
        pl.BlockSpec((1, 1, tiles_per_page, LANES), lambda s, g, pt: (s, n_pages, 0, 0)),
        pl.BlockSpec((1, rows, 1), lambda s, g, pt: (s, 0, 0)),
        pl.BlockSpec((1, LANES, hd), lambda s, g, pt: (s, 0, 0)),
        pl.BlockSpec((1, LANES, hd), lambda s, g, pt: (s, 0, 0)),
        pl.BlockSpec((1, rows, hd), lambda s, g, pt: (s, 0, 0)),
    ]
    vmem = 2 * 2 * group * page * heads * hd * 4 + 4 * rows * page * heads * 4 + (4 << 20)
    return pl.pallas_call(
        functools.partial(_attn_sample_body, group=group, heads=heads, dec_seq=dec_seq, n_groups=n_groups),
        grid_spec=pltpu.PrefetchScalarGridSpec(
            num_scalar_prefetch=1, grid=(db, n_groups), in_specs=in_specs,
            out_specs=pl.BlockSpec((1, rows, hd), lambda s, g, pt: (s, 0, 0)),
            scratch_shapes=[pltpu.VMEM((rows, 1), F32), pltpu.VMEM((rows, 1), F32), pltpu.VMEM((rows, hd), F32)]),
        out_shape=jax.ShapeDtypeStruct((db, rows, hd), F32),
        compiler_params=_params(vmem, 2),
        name="attn_sample",
    )(page_table, *([cache_k] * group), *([cache_v] * group), q_rows, c_all, c_all, cq_col, kn_rows, vn_rows,
      sg_rows)


def _pick(n, pref):
    t = min(pref, n)
    assert n % t == 0, (n, t)
    return t


def _layer(xp, xs, cache_k, cache_v, logf_pages, state_conv_l, page_table, layer,
           g_pre, w_t, b_f, w_conv, w_out, g_post):
    b, s, d = xp.shape
    db, ds, _ = xs.shape
    heads = b_f.shape[0]
    da = heads * HEAD_DIM
    dc = w_conv.shape[1]
    page = cache_k.shape[2]
    n_pages = page_table.shape[1]
    mp, msz = b * s, db * ds
    scale = HEAD_DIM ** -0.5 * LOG2E

    tc = _pick(dc, 256)
    n_ct = dc // tc
    off = 3 * da + heads
    assert w_t.shape[1] == off + da + 4 * dc and _pick(4 * dc, 256) == tc
    wqkv = _weights_call(w_t, layer, 0, 3 * da, lambda i: i, "weights_qkv")
    wga = _weights_call(w_t, layer, off, da, lambda i: i, "weights_gate")
    w4 = _weights_call(w_t, layer, off + da, 4 * dc, lambda i: 4 * (i % n_ct) + i // n_ct, "weights_conv")
    wf = jnp.pad(w_t[layer, 3 * da:off], ((0, LANES - heads), (0, 0))).astype(BF16)
    bf = jnp.pad(b_f, (0, LANES - heads)).reshape(1, LANES).astype(F32)
    wo = w_out.astype(BF16)
    gpre = g_pre.reshape(1, d)
    gpost = g_post.reshape(1, d)

    def project(x2d, tm):
        xn = _rmsnorm_bf16(x2d, gpre)
        tn = _pick(da, 1024)
        (qv,) = _proj_call(functools.partial(_q_body, scale=scale), xn, wqkv, [BF16], tm, tn, "proj_q", da, 0)
        k32, k16 = _proj_call(_kv_body, xn, wqkv, [F32, BF16], tm, tn, "proj_k", da, da)
        v32, v16 = _proj_call(_kv_body, xn, wqkv, [F32, BF16], tm, tn, "proj_v", da, 2 * da)
        (sg,) = _proj_call(_gate_body, xn, wga, [BF16], tm, tn, "proj_gate")
        return xn, qv, k32, k16, v32, v16, sg

    xp2 = xp.reshape(mp, d)
    tm_p = _pick(s, 512)
    xn, qv, k32, k16, v32, v16, sg = project(xp2, tm_p)
    logf_pad, c_pad = _forget_call(xn, wf, bf, s // tm_p, tm_p)
    logf_p = logf_pad[:, :heads].reshape(b, s, heads)
    c_p = c_pad[:, :heads].reshape(b, s, heads)
    c_row = c_p.transpose(0, 2, 1).reshape(b, heads, 1, s)
    blk = _pick(s, 512)
    ma_p = _attn_prompt_call(qv.reshape(b, s, da), k16.reshape(b, s, da), v16.reshape(b, s, da),
                             c_row, sg.reshape(b, s, da), blk)
    mc_p, conv_p = _conv_prompt_call(xn, w4, w_conv, b, tm_p, tc)
    tm_o = _pick(s, 512)
    yp = _out_call(ma_p.reshape(mp, da), mc_p, wo, xp2, gpost, tm_o, _pick(d, 512)).reshape(b, s, d)
    k_p = k32.reshape(b, s, heads, HEAD_DIM)
    v_p = v32.reshape(b, s, heads, HEAD_DIM)

    xs2 = xs.reshape(msz, d)
    tm_s = _pick(msz, 512)
    xn, qv, k32, k16, v32, v16, sg = project(xs2, tm_s)
    logf_pad, _ = _forget_call(xn, wf, bf, 1, tm_s)
    logf_s = logf_pad[:, :heads].reshape(db, ds, heads)
    rows = ds * heads
    tiles_per_page = page * heads // LANES
    new_rows = jnp.pad(logf_s.reshape(db, 1, rows), ((0, 0), (0, tiles_per_page - 1), (0, LANES - rows)))
    c_all = _cum_call(page_table, logf_pages, layer, new_rows, heads)
    cq_s = c_all[:, n_pages, 0, :rows].reshape(db, rows, 1)
    pad_rows = ((0, 0), (0, LANES - rows), (0, 0))
    kn = jnp.pad(k16.reshape(db, rows, HEAD_DIM), pad_rows)
    vn = jnp.pad(v16.reshape(db, rows, HEAD_DIM), pad_rows)
    o_s = _attn_sample_call(page_table, cache_k, cache_v, layer, qv.reshape(db, rows, HEAD_DIM), c_all, cq_s,
                            kn, vn, sg.reshape(db, rows, HEAD_DIM), _pick(n_pages, 8))
    ma_s = o_s.reshape(msz, da).astype(BF16)
    hist = state_conv_l
    zeros = jnp.zeros((db, 1, dc), F32)
    h1 = jnp.concatenate([hist[:, 1:2]] + [zeros] * (ds - 1), axis=1).reshape(msz, dc)
    h2 = jnp.concatenate([hist[:, 0:1], hist[:, 1:2]] + [zeros] * (ds - 2), axis=1).reshape(msz, dc)
    mc_s, u_s = _conv_sample_call(xn, w4, w_conv, h1, h2, ds, tc)
    ys = _out_call(ma_s, mc_s, wo, xs2, gpost, _pick(msz, 512), _pick(d, 512)).reshape(db, ds, d)
    conv_s = u_s.reshape(db, ds, dc)[:, ds - (CONV_W - 1):]
    k_s = k32.reshape(db, ds, heads, HEAD_DIM)
    v_s = v32.reshape(db, ds, heads, HEAD_DIM)
    return yp, ys, (k_p, v_p, logf_p, conv_p, k_s, v_s, logf_s, conv_s)


def kernel(x_prompt, x_sample, cache_k, cache_v, cache_logf, state_conv, page_table, g_pre, w_in, b_f, w_conv,
           w_out, g_post):
    depth, n_pool, page, heads = cache_logf.shape
    assert cache_k.shape[-1] == HEAD_DIM and (page * heads) % LANES == 0 and LANES % heads == 0
    assert x_sample.shape[1] >= CONV_W - 1 and x_sample.shape[1] * heads <= LANES
    logf_pages = cache_logf.reshape(depth, n_pool, page * heads // LANES, LANES)
    w_t = jnp.swapaxes(w_in, 1, 2)
    xp, xs = x_prompt, x_sample
    per_layer = []
    for layer in range(depth):
        xp, xs, outs = _layer(xp, xs, cache_k, cache_v, logf_pages, state_conv[layer], page_table, layer,
                              g_pre[layer], w_t, b_f[layer], w_conv[layer], w_out[layer], g_post[layer])
        per_layer.append(outs)
    stacked = [jnp.stack(leaf) for leaf in zip(*per_layer)]
    return (xp, xs, *stacked)
```

```python
import functools

import jax
import jax.numpy as jnp
from jax import lax
from jax.experimental import pallas as pl
from jax.experimental.pallas import tpu as pltpu

HEAD_DIM = 128
CONV_W = 3
EPS = 1e-6
LANES = 128
SUBLANES = 8
V7X_VMEM_BYTES = 64 * 1024 * 1024
VMEM_BUDGET_BYTES = 58 * 1024 * 1024
F32 = jnp.float32
BF16 = jnp.bfloat16
NEG_INF = float("-inf")
LOG2E = 1.4426950408889634
UNROLL_KEY_TILES = 2
RING_SLOTS = 4


def _params(vmem_bytes, n_grid):
    limit = min(int(vmem_bytes * 1.25) + (4 << 20), VMEM_BUDGET_BYTES)
    return pltpu.CompilerParams(dimension_semantics=("arbitrary",) * n_grid, vmem_limit_bytes=limit)


def _silu(z):
    return z * (1.0 / (1.0 + jnp.exp(-z)))


def _split3(x):
    hi = x.astype(BF16)
    r = x - hi.astype(F32)
    mid = r.astype(BF16)
    lo = (r - mid.astype(F32)).astype(BF16)
    return hi, mid, lo


def _dot_f32_by_01(x, w01):
    w = w01.astype(BF16)
    hi, mid, lo = (jnp.dot(p, w, preferred_element_type=F32) for p in _split3(x))
    return (lo + mid) + hi


def _dot_01_by_f32(w01, x):
    w = w01.astype(BF16)
    hi, mid, lo = (jnp.dot(w, p, preferred_element_type=F32) for p in _split3(x))
    return (lo + mid) + hi


def _rmsnorm_kernel(x_ref, g_ref, o_ref):
    x = x_ref[...]
    ms = jnp.mean(x * x, axis=-1, keepdims=True)
    o_ref[...] = (x * lax.rsqrt(ms + EPS) * g_ref[...]).astype(o_ref.dtype)


def _rmsnorm_bf16(x2d, g_row):
    m, d = x2d.shape
    tm = min(256, m)
    return pl.pallas_call(
        _rmsnorm_kernel,
        grid=(m // tm,),
        in_specs=[pl.BlockSpec((tm, d), lambda i: (i, 0)), pl.BlockSpec((1, d), lambda i: (0, 0))],
        out_specs=pl.BlockSpec((tm, d), lambda i: (i, 0)),
        out_shape=jax.ShapeDtypeStruct((m, d), BF16),
        compiler_params=_params(2 * tm * d * 6, 1),
        name="rmsnorm_pre",
    )(x2d, g_row)


def _weights_body(w_ref, o_ref):
    o_ref[...] = w_ref[0].T.astype(BF16)


def _weights_call(w_t, layer, row0, n_rows, out_block_of, name):
    d = w_t.shape[2]
    tr = _pick(n_rows, 256)
    assert row0 % SUBLANES == 0
    return pl.pallas_call(
        _weights_body,
        grid=(n_rows // tr,),
        in_specs=[pl.BlockSpec((pl.Element(1), pl.Element(tr), pl.Element(d)), lambda i: (layer, pl.multiple_of(row0 + i * tr, SUBLANES), 0))],
        out_specs=pl.BlockSpec((d, tr), lambda i: (0, out_block_of(i))),
        out_shape=jax.ShapeDtypeStruct((d, n_rows), BF16),
        compiler_params=_params(2 * tr * d * 6 + tr * d * 8, 1),
        name=name,
    )(w_t)


def _proj_call(body, xn, w, out_dtypes, tm, tn, name, n=None, col0=0):
    m, d = xn.shape
    n = w.shape[1] if n is None else n
    assert col0 % tn == 0 and n % tn == 0
    jb = col0 // tn
    out_bytes = sum(jnp.dtype(t).itemsize for t in out_dtypes)
    vmem = 2 * (tm * d * 2 + d * tn * 2 + tm * tn * out_bytes) + tm * tn * 4
    return pl.pallas_call(
        body,
        grid=(n // tn, m // tm),
        in_specs=[pl.BlockSpec((tm, d), lambda j, i: (i, 0)), pl.BlockSpec((d, tn), lambda j, i: (0, j + jb))],
        out_specs=[pl.BlockSpec((tm, tn), lambda j, i: (i, j)) for _ in out_dtypes],
        out_shape=[jax.ShapeDtypeStruct((m, n), t) for t in out_dtypes],
        compiler_params=_params(vmem, 2),
        name=name,
    )(xn, w)


def _q_body(xn_ref, w_ref, o_ref, *, scale):
    z = jnp.dot(xn_ref[...], w_ref[...], preferred_element_type=F32)
    o_ref[...] = (z * scale).astype(o_ref.dtype)


def _kv_body(xn_ref, w_ref, o32_ref, o16_ref):
    z = jnp.dot(xn_ref[...], w_ref[...], preferred_element_type=F32)
    o32_ref[...] = z
    o16_ref[...] = z.astype(BF16)


def _gate_body(xn_ref, w_ref, o_ref):
    z = jnp.dot(xn_ref[...], w_ref[...], preferred_element_type=F32)
    o_ref[...] = _silu(z).astype(o_ref.dtype)


def _log_sigmoid(z):
    return jnp.minimum(z, 0.0) - jnp.log1p(jnp.exp(-jnp.abs(z)))


def _forget_body(xn_ref, w_ref, b_ref, logf_ref, c_ref, carry_ref, *, tiles_per_seq):
    i = pl.program_id(0)
    tm = xn_ref.shape[0]
    z = lax.dot_general(xn_ref[...], w_ref[...], (((1,), (1,)), ((), ())), preferred_element_type=F32)
    logf = _log_sigmoid(z + b_ref[...])
    logf_ref[...] = logf

    @pl.when(i % tiles_per_seq == 0)
    def _():
        carry_ref[...] = jnp.zeros_like(carry_ref)

    row = lax.broadcasted_iota(jnp.int32, (tm, tm), 0)
    col = lax.broadcasted_iota(jnp.int32, (tm, tm), 1)
    c = _dot_01_by_f32(col <= row, logf) + carry_ref[0:1, :]
    c_ref[...] = c
    carry_ref[...] = jnp.broadcast_to(c[tm - 1:tm, :], carry_ref.shape)


def _forget_call(xn, wf_pad, bf_pad, tiles_per_seq, tm):
    m, d = xn.shape
    vmem = 2 * (tm * d * 2 + d * LANES * 2 + 2 * tm * LANES * 4) + tm * tm * 8
    return pl.pallas_call(
        functools.partial(_forget_body, tiles_per_seq=tiles_per_seq),
        grid=(m // tm,),
        in_specs=[pl.BlockSpec((tm, d), lambda i: (i, 0)),
                  pl.BlockSpec((LANES, d), lambda i: (0, 0)),
                  pl.BlockSpec((1, LANES), lambda i: (0, 0))],
        out_specs=[pl.BlockSpec((tm, LANES), lambda i: (i, 0)), pl.BlockSpec((tm, LANES), lambda i: (i, 0))],
        out_shape=[jax.ShapeDtypeStruct((m, LANES), F32), jax.ShapeDtypeStruct((m, LANES), F32)],
        scratch_shapes=[pltpu.VMEM((SUBLANES, LANES), F32)],
        compiler_params=_params(vmem, 1),
        name="forget_gate",
    )(xn, wf_pad, bf_pad)


def _conv_tail(zb, zg, u, u1, u2, wc_ref, m_ref):
    wc = wc_ref[...]
    y = wc[0:1, :] * u2 + wc[1:2, :] * u1 + wc[2:3, :] * u
    m_ref[...] = (zb * y * _silu(zg)).astype(m_ref.dtype)


def _conv_prompt_body(xn_ref, w_ref, wc_ref, m_ref, cn_ref, carry_ref, *, tiles_per_seq, tc):
    i = pl.program_id(1)
    tm = xn_ref.shape[0]
    z = jnp.dot(xn_ref[...], w_ref[...], preferred_element_type=F32)
    zb, zc, zh, zg = (z[:, k * tc:(k + 1) * tc] for k in range(4))
    u = zc * zh

    @pl.when(i % tiles_per_seq == 0)
    def _():
        carry_ref[...] = jnp.zeros_like(carry_ref)

    prev = carry_ref[...]
    p1 = prev[SUBLANES - 1:SUBLANES, :]
    p2 = prev[SUBLANES - 2:SUBLANES - 1, :]
    row = lax.broadcasted_iota(jnp.int32, (tm, tc), 0)
    u1 = jnp.where(row >= 1, pltpu.roll(u, 1, axis=0), p1)
    u2 = jnp.where(row >= 2, pltpu.roll(u, 2, axis=0), jnp.where(row == 1, p1, p2))
    carry_ref[...] = u[tm - SUBLANES:tm, :]
    cn_ref[0] = u[tm - (CONV_W - 1):tm, :]
    _conv_tail(zb, zg, u, u1, u2, wc_ref, m_ref)


def _conv_sample_body(xn_ref, w_ref, wc_ref, h1_ref, h2_ref, m_ref, u_ref, *, dec_seq, tc):
    tm = xn_ref.shape[0]
    z = jnp.dot(xn_ref[...], w_ref[...], preferred_element_type=F32)
    zb, zc, zh, zg = (z[:, k * tc:(k + 1) * tc] for k in range(4))
    u = zc * zh
    u_ref[...] = u
    t = lax.rem(lax.broadcasted_iota(jnp.int32, (tm, tc), 0), dec_seq)
    u1 = jnp.where(t >= 1, pltpu.roll(u, 1, axis=0), h1_ref[...])
    u2 = jnp.where(t >= 2, pltpu.roll(u, 2, axis=0), h2_ref[...])
    _conv_tail(zb, zg, u, u1, u2, wc_ref, m_ref)


def _conv_prompt_call(xn, w4, wconv, batch, tm, tc):
    m, d = xn.shape
    dc = wconv.shape[1]
    tiles_per_seq = m // batch // tm
    vmem = 2 * (tm * d * 2 + d * 4 * tc * 2 + tm * tc * 2) + tm * 4 * tc * 4 * 2
    return pl.pallas_call(
        functools.partial(_conv_prompt_body, tiles_per_seq=tiles_per_seq, tc=tc),
        grid=(dc // tc, m // tm),
        in_specs=[pl.BlockSpec((tm, d), lambda j, i: (i, 0)),
                  pl.BlockSpec((d, 4 * tc), lambda j, i: (0, j)),
                  pl.BlockSpec((CONV_W, tc), lambda j, i: (0, j))],
        out_specs=[pl.BlockSpec((tm, tc), lambda j, i: (i, j)),
                   pl.BlockSpec((1, CONV_W - 1, tc), lambda j, i: (i // tiles_per_seq, 0, j))],
        out_shape=[jax.ShapeDtypeStruct((m, dc), BF16), jax.ShapeDtypeStruct((batch, CONV_W - 1, dc), F32)],
        scratch_shapes=[pltpu.VMEM((SUBLANES, tc), F32)],
        compiler_params=_params(vmem, 2),
        name="conv_prompt",
    )(xn, w4, wconv)


def _conv_sample_call(xn, w4, wconv, h1, h2, dec_seq, tc):
    m, d = xn.shape
    dc = wconv.shape[1]
    vmem = 2 * (m * d * 2 + d * 4 * tc * 2 + m * tc * 14) + m * 4 * tc * 4 * 2
    return pl.pallas_call(
        functools.partial(_conv_sample_body, dec_seq=dec_seq, tc=tc),
        grid=(dc // tc,),
        in_specs=[pl.BlockSpec((m, d), lambda j: (0, 0)),
                  pl.BlockSpec((d, 4 * tc), lambda j: (0, j)),
                  pl.BlockSpec((CONV_W, tc), lambda j: (0, j)),
                  pl.BlockSpec((m, tc), lambda j: (0, j)),
                  pl.BlockSpec((m, tc), lambda j: (0, j))],
        out_specs=[pl.BlockSpec((m, tc), lambda j: (0, j)), pl.BlockSpec((m, tc), lambda j: (0, j))],
        out_shape=[jax.ShapeDtypeStruct((m, dc), BF16), jax.ShapeDtypeStruct((m, dc), F32)],
        compiler_params=_params(vmem, 1),
        name="conv_sample",
    )(xn, w4, wconv, h1, h2)


def _out_body(ma_ref, mc_ref, wa_ref, wc_ref, x_ref, g_ref, o_ref, *, tn, n_tiles):
    j = pl.program_id(1)
    z = (jnp.dot(ma_ref[...], wa_ref[...], preferred_element_type=F32)
         + jnp.dot(mc_ref[...], wc_ref[...], preferred_element_type=F32))
    o_ref[:, pl.ds(pl.multiple_of(j * tn, tn), tn)] = z

    @pl.when(j == n_tiles - 1)
    def _():
        zz = o_ref[...]
        ms = jnp.mean(zz * zz, axis=-1, keepdims=True)
        o_ref[...] = x_ref[...] + zz * lax.rsqrt(ms + EPS) * g_ref[...]


def _out_call(ma, mc, wo, x2d, g_row, tm, tn):
    m, da = ma.shape
    dcv = mc.shape[1]
    d = x2d.shape[1]
    assert da == dcv, "w_out row blocks are indexed in units of the attention width"
    n_tiles = d // tn
    vmem = 2 * (2 * tm * d * 4 + tm * (da + dcv) * 2 + (da + dcv) * tn * 2) + tm * tn * 4
    return pl.pallas_call(
        functools.partial(_out_body, tn=tn, n_tiles=n_tiles),
        grid=(m // tm, n_tiles),
        in_specs=[pl.BlockSpec((tm, da), lambda i, j: (i, 0)),
                  pl.BlockSpec((tm, dcv), lambda i, j: (i, 0)),
                  pl.BlockSpec((da, tn), lambda i, j: (0, j)),
                  pl.BlockSpec((dcv, tn), lambda i, j: (1, j)),
                  pl.BlockSpec((tm, d), lambda i, j: (i, 0)),
                  pl.BlockSpec((1, d), lambda i, j: (0, 0))],
        out_specs=pl.BlockSpec((tm, d), lambda i, j: (i, 0)),
        out_shape=jax.ShapeDtypeStruct((m, d), F32),
        compiler_params=_params(vmem, 2),
        name="out_proj",
    )(ma, mc, wo, wo, x2d, g_row)


def _attn_prompt_body(q_ref, k_ref, v_ref, c_ref, sg_ref, o_ref, vt_ref, crep_ref, *, blk, qsub):
    qi = pl.program_id(2)
    seq = k_ref.shape[1]

    @pl.when(qi == 0)
    def _():
        for c0 in range(0, seq, blk):
            vt_ref[:, c0:c0 + blk] = v_ref[0, c0:c0 + blk, :].astype(F32).T.astype(BF16)
            row = c_ref[0, 0, :, c0:c0 + blk] * LOG2E
            crep_ref[c0:c0 + blk, :] = jnp.broadcast_to(row, (LANES, blk)).T

    n_sub = blk // qsub
    q_subs = [q_ref[0, j * qsub:(j + 1) * qsub, :].astype(F32).T.astype(BF16) for j in range(n_sub)]
    c_q = c_ref[0, 0, :, pl.ds(pl.multiple_of(qi * blk, blk), blk)] * LOG2E
    cq_subs = [c_q[:, j * qsub:(j + 1) * qsub] for j in range(n_sub)]

    def tile(kj, carry, diagonal):
        ks = pl.multiple_of(kj * blk, blk)
        k = k_ref[0, pl.ds(ks, blk), :]
        vt = vt_ref[:, pl.ds(ks, blk)]
        c_k = jnp.tile(crep_ref[pl.ds(ks, blk), :], (1, qsub // LANES))
        out = []
        for j, (m, l, acc) in enumerate(carry):
            st = jnp.dot(k, q_subs[j], preferred_element_type=F32) - c_k
            if diagonal:
                key = lax.broadcasted_iota(jnp.int32, (blk, qsub), 0)
                qry = lax.broadcasted_iota(jnp.int32, (blk, qsub), 1) + j * qsub
                st = jnp.where(key <= qry, st, NEG_INF)
            m_new = jnp.maximum(m, jnp.max(st, axis=0, keepdims=True) + cq_subs[j])
            alpha = jnp.exp2(m - m_new)
            pt = jnp.exp2(st - (m_new - cq_subs[j]))
            l = alpha * l + jnp.sum(pt, axis=0, keepdims=True)
            acc = alpha * acc + jnp.dot(vt, pt.astype(BF16), preferred_element_type=F32)
            out.append((m_new, l, acc))
        return tuple(out)

    init = tuple((jnp.full((1, qsub), NEG_INF, F32), jnp.zeros((1, qsub), F32), jnp.zeros((HEAD_DIM, qsub), F32))
                 for _ in range(n_sub))
    def multi(p, c):
        for u in range(UNROLL_KEY_TILES):
            c = tile(UNROLL_KEY_TILES * p + u, c, False)
        return c

    n_multi = qi // UNROLL_KEY_TILES
    carry = lax.fori_loop(0, n_multi, multi, init)
    carry = lax.fori_loop(n_multi * UNROLL_KEY_TILES, qi, lambda kj, c: tile(kj, c, False), carry)
    for j, (_, l, acc) in enumerate(tile(qi, carry, True)):
        o_ref[0, j * qsub:(j + 1) * qsub, :] = (
            (acc * (1.0 / l)).T * sg_ref[0, j * qsub:(j + 1) * qsub, :].astype(F32)).astype(o_ref.dtype)


def _attn_prompt_call(q, k, v, c_row, sg, blk):
    b, s, da = q.shape
    h = da // HEAD_DIM
    vmem = (2 * (2 * s * HEAD_DIM * 2 + 3 * blk * HEAD_DIM * 2 + s * 4 * SUBLANES)
            + s * HEAD_DIM * 2 + s * LANES * 4 + 6 * blk * blk * 4)
    return pl.pallas_call(
        functools.partial(_attn_prompt_body, blk=blk, qsub=blk),
        grid=(b, h, s // blk),
        in_specs=[pl.BlockSpec((1, blk, HEAD_DIM), lambda bi, hi, qi: (bi, qi, hi)),
                  pl.BlockSpec((1, s, HEAD_DIM), lambda bi, hi, qi: (bi, 0, hi)),
                  pl.BlockSpec((1, s, HEAD_DIM), lambda bi, hi, qi: (bi, 0, hi)),
                  pl.BlockSpec((1, 1, 1, s), lambda bi, hi, qi: (bi, hi, 0, 0)),
                  pl.BlockSpec((1, blk, HEAD_DIM), lambda bi, hi, qi: (bi, qi, hi))],
        out_specs=pl.BlockSpec((1, blk, HEAD_DIM), lambda bi, hi, qi: (bi, qi, hi)),
        out_shape=jax.ShapeDtypeStruct((b, s, da), BF16),
        scratch_shapes=[pltpu.VMEM((HEAD_DIM, s), BF16), pltpu.VMEM((s, LANES), F32)],
        compiler_params=_params(vmem, 3),
        name="attn_prompt",
    )(q, k, v, c_row, sg)


def _cum_body(pt_ref, *refs, n_pages, heads):
    del pt_ref
    x_refs, c_ref = refs[:n_pages + 1], refs[n_pages + 1]
    rows = x_refs[0].shape[-2]
    n_valid = (n_pages + 1) * rows
    n = -(-n_valid // LANES) * LANES
    blocks = [r[...] for r in x_refs]
    if n > n_valid:
        blocks.append(jnp.zeros((n - n_valid, LANES), F32))
    x = jnp.concatenate(blocks, axis=0)
    l1 = lax.broadcasted_iota(jnp.int32, (LANES, LANES), 0)
    l2 = lax.broadcasted_iota(jnp.int32, (LANES, LANES), 1)
    same_head = (l1 % heads) == (l2 % heads)
    within = same_head & ((l1 // heads) <= (l2 // heads))
    r1 = lax.broadcasted_iota(jnp.int32, (n, n), 0)
    r2 = lax.broadcasted_iota(jnp.int32, (n, n), 1)
    c = _dot_f32_by_01(x, within) + _dot_01_by_f32(r2 < r1, _dot_f32_by_01(x, same_head))
    c_ref[0] = c[:n_valid].reshape(n_pages + 1, rows, LANES)


def _cum_call(page_table, logf_pages, layer, new_rows, heads):
    db, n_pages = page_table.shape
    rows = logf_pages.shape[2]

    def page_map(p):
        return lambda s, pt: (layer, pt[s, p], 0, 0)

    in_specs = [pl.BlockSpec((None, None, rows, LANES), page_map(p)) for p in range(n_pages)]
    in_specs.append(pl.BlockSpec((None, rows, LANES), lambda s, pt: (s, 0, 0)))
    n = (n_pages + 1) * rows
    return pl.pallas_call(
        functools.partial(_cum_body, n_pages=n_pages, heads=heads),
        grid_spec=pltpu.PrefetchScalarGridSpec(
            num_scalar_prefetch=1, grid=(db,), in_specs=in_specs,
            out_specs=pl.BlockSpec((1, n_pages + 1, rows, LANES), lambda s, pt: (s, 0, 0, 0))),
        out_shape=jax.ShapeDtypeStruct((db, n_pages + 1, rows, LANES), F32),
        compiler_params=_params(4 * n * LANES * 4 + n * n * 8, 1),
        name="cache_logf_cumsum",
    )(page_table, *([logf_pages] * n_pages), new_rows)


def _attn_sample_body(pt_ref, kc_ref, vc_ref, q_ref, c_ref, cn_ref, cq_ref, kn_ref, vn_ref, sg_ref, o_ref,
                      kbuf, vbuf, sem, m_ref, l_ref, acc_ref, *, layer, group, heads, dec_seq, n_groups, n_slots):
    g = pl.program_id(1)
    step = pl.program_id(0) * n_groups + g
    n_steps = pl.num_programs(0) * n_groups
    rows = heads * dec_seq
    q = q_ref[0]
    cq = cq_ref[0] * LOG2E

    def page_copies(t, slot):
        seq = t // n_groups
        first = (t - seq * n_groups) * group
        copies = []
        for i in range(group):
            page = pt_ref[seq, first + i]
            copies.append(pltpu.make_async_copy(kc_ref.at[layer, page], kbuf.at[slot, i], sem.at[0, slot]))
            copies.append(pltpu.make_async_copy(vc_ref.at[layer, page], vbuf.at[slot, i], sem.at[1, slot]))
        return copies

    @pl.when(step == 0)
    def _():
        for t in range(n_slots - 1):
            for cp in page_copies(t, t):
                cp.start()

    ahead = step + (n_slots - 1)

    @pl.when(ahead < n_steps)
    def _():
        for cp in page_copies(ahead, ahead % n_slots):
            cp.start()

    slot = step % n_slots
    for cp in page_copies(step, slot):
        cp.wait()

    @pl.when(g == 0)
    def _():
        m_ref[...] = jnp.full(m_ref.shape, NEG_INF, F32)
        l_ref[...] = jnp.zeros_like(l_ref)
        acc_ref[...] = jnp.zeros_like(acc_ref)

    r_head = lax.broadcasted_iota(jnp.int32, (rows, LANES), 0) % heads
    lane = lax.broadcasted_iota(jnp.int32, (rows, LANES), 1)
    same_head = (lane % heads) == r_head

    def local(kf, vf, bias_tiles):
        s = lax.dot_general(q, kf, (((1,), (1,)), ((), ())), preferred_element_type=F32)
        s = jnp.concatenate([s[:, t * LANES:(t + 1) * LANES] + bias_tiles[t] for t in range(len(bias_tiles))], axis=1)
        m_blk = jnp.max(s, axis=-1, keepdims=True)
        p = jnp.exp2(s - m_blk)
        return m_blk, jnp.sum(p, axis=-1, keepdims=True), jnp.dot(p.astype(BF16), vf, preferred_element_type=F32)

    def merge(parts):
        m, l, acc = m_ref[...], l_ref[...], acc_ref[...]
        for m_blk, l_blk, o_blk in parts:
            m_new = jnp.maximum(m, m_blk + cq)
            a_old, a_blk = jnp.exp2(m - m_new), jnp.exp2(m_blk + cq - m_new)
            l = a_old * l + a_blk * l_blk
            acc = a_old * acc + a_blk * o_blk
            m = m_new
        m_ref[...], l_ref[...], acc_ref[...] = m, l, acc

    head_mask = jnp.where(same_head, 0.0, NEG_INF).astype(F32)
    tiles_per_page = kbuf.shape[2] * heads // LANES
    parts = []
    for i in range(group):
        kf = kbuf[slot, i].reshape(-1, HEAD_DIM).astype(BF16)
        vf = vbuf[slot, i].reshape(-1, HEAD_DIM).astype(BF16)
        c = c_ref[0, i] * LOG2E
        parts.append(local(kf, vf, [head_mask - c[t:t + 1, :] for t in range(tiles_per_page)]))
    merge(parts)

    @pl.when(g == n_groups - 1)
    def _():
        r_query = lax.broadcasted_iota(jnp.int32, (rows, LANES), 0) // heads
        ok = same_head & ((lane // heads) <= r_query) & (lane < rows)
        bias = jnp.where(ok, 0.0, NEG_INF).astype(F32) - cn_ref[0, 0][0:1, :] * LOG2E
        merge([local(kn_ref[0], vn_ref[0], [bias])])
        o_ref[0] = acc_ref[...] * (1.0 / l_ref[...]) * sg_ref[0].astype(F32)


def _attn_sample_call(page_table, cache_k, cache_v, layer, q_rows, c_all, cq_col, kn_rows, vn_rows, sg_rows,
                      group):
    db, n_pages = page_table.shape
    _, _, page, heads, hd = cache_k.shape
    rows = q_rows.shape[1]
    dec_seq = rows // heads
    n_groups = n_pages // group
    tiles_per_page = page * heads // LANES

    n_slots = RING_SLOTS
    assert db * n_groups >= n_slots - 1
    in_specs = [
        pl.BlockSpec(memory_space=pl.ANY),
        pl.BlockSpec(memory_space=pl.ANY),
        pl.BlockSpec((1, rows, hd), lambda s, g, pt: (s, 0, 0)),
        pl.BlockSpec((1, group, tiles_per_page, LANES), lambda s, g, pt: (s, g, 0, 0)),
        pl.BlockSpec((1, 1, tiles_per_page, LANES), lambda s, g, pt: (s, n_pages, 0, 0)),
        pl.BlockSpec((1, rows, 1), lambda s, g, pt: (s, 0, 0)),
        pl.BlockSpec((1, LANES, hd), lambda s, g, pt: (s, 0, 0)),
        pl.BlockSpec((1, LANES, hd), lambda s, g, pt: (s, 0, 0)),
        pl.BlockSpec((1, rows, hd), lambda s, g, pt: (s, 0, 0)),
    ]
    ring = (n_slots, group, page, heads, hd)
    vmem = 2 * n_slots * group * page * heads * hd * 4 + 4 * rows * page * heads * 4 + (4 << 20)
    return pl.pallas_call(
        functools.partial(_attn_sample_body, layer=layer, group=group, heads=heads, dec_seq=dec_seq,
                          n_groups=n_groups, n_slots=n_slots),
        grid_spec=pltpu.PrefetchScalarGridSpec(
            num_scalar_prefetch=1, grid=(db, n_groups), in_specs=in_specs,
            out_specs=pl.BlockSpec((1, rows, hd), lambda s, g, pt: (s, 0, 0)),
            scratch_shapes=[pltpu.VMEM(ring, F32), pltpu.VMEM(ring, F32), pltpu.SemaphoreType.DMA((2, n_slots)),
                            pltpu.VMEM((rows, 1), F32), pltpu.VMEM((rows, 1), F32), pltpu.VMEM((rows, hd), F32)]),
        out_shape=jax.ShapeDtypeStruct((db, rows, hd), F32),
        compiler_params=_params(vmem, 2),
        name="attn_sample",
    )(page_table, cache_k, cache_v, q_rows, c_all, c_all, cq_col, kn_rows, vn_rows, sg_rows)


def _pick(n, pref):
    t = min(pref, n)
    assert n % t == 0, (n, t)
    return t


def _layer(xp, xs, cache_k, cache_v, logf_pages, state_conv_l, page_table, layer,
           g_pre, w_t, b_f, w_conv, w_out, g_post):
    b, s, d = xp.shape
    db, ds, _ = xs.shape
    heads = b_f.shape[0]
    da = heads * HEAD_DIM
    dc = w_conv.shape[1]
    page = cache_k.shape[2]
    n_pages = page_table.shape[1]
    mp, msz = b * s, db * ds
    scale = HEAD_DIM ** -0.5 * LOG2E

    tc = _pick(dc, 256)
    n_ct = dc // tc
    off = 3 * da + heads
    assert w_t.shape[1] == off + da + 4 * dc and _pick(4 * dc, 256) == tc
    wqkv = _weights_call(w_t, layer, 0, 3 * da, lambda i: i, "weights_qkv")
    wga = _weights_call(w_t, layer, off, da, lambda i: i, "weights_gate")
    w4 = _weights_call(w_t, layer, off + da, 4 * dc, lambda i: 4 * (i % n_ct) + i // n_ct, "weights_conv")
    wf = jnp.pad(w_t[layer, 3 * da:off], ((0, LANES - heads), (0, 0))).astype(BF16)
    bf = jnp.pad(b_f, (0, LANES - heads)).reshape(1, LANES).astype(F32)
    wo = w_out.astype(BF16)
    gpre = g_pre.reshape(1, d)
    gpost = g_post.reshape(1, d)

    def project(x2d, tm):
        xn = _rmsnorm_bf16(x2d, gpre)
        tn = _pick(da, 1024)
        (qv,) = _proj_call(functools.partial(_q_body, scale=scale), xn, wqkv, [BF16], tm, tn, "proj_q", da, 0)
        k32, k16 = _proj_call(_kv_body, xn, wqkv, [F32, BF16], tm, tn, "proj_k", da, da)
        v32, v16 = _proj_call(_kv_body, xn, wqkv, [F32, BF16], tm, tn, "proj_v", da, 2 * da)
        (sg,) = _proj_call(_gate_body, xn, wga, [BF16], tm, tn, "proj_gate")
        return xn, qv, k32, k16, v32, v16, sg

    xp2 = xp.reshape(mp, d)
    tm_p = _pick(s, 512)
    xn, qv, k32, k16, v32, v16, sg = project(xp2, tm_p)
    logf_pad, c_pad = _forget_call(xn, wf, bf, s // tm_p, tm_p)
    logf_p = logf_pad[:, :heads].reshape(b, s, heads)
    c_p = c_pad[:, :heads].reshape(b, s, heads)
    c_row = c_p.transpose(0, 2, 1).reshape(b, heads, 1, s)
    blk = _pick(s, 512)
    ma_p = _attn_prompt_call(qv.reshape(b, s, da), k16.reshape(b, s, da), v16.reshape(b, s, da),
                             c_row, sg.reshape(b, s, da), blk)
    mc_p, conv_p = _conv_prompt_call(xn, w4, w_conv, b, tm_p, tc)
    tm_o = _pick(s, 512)
    yp = _out_call(ma_p.reshape(mp, da), mc_p, wo, xp2, gpost, tm_o, _pick(d, 512)).reshape(b, s, d)
    k_p = k32.reshape(b, s, heads, HEAD_DIM)
    v_p = v32.reshape(b, s, heads, HEAD_DIM)

    xs2 = xs.reshape(msz, d)
    tm_s = _pick(msz, 512)
    xn, qv, k32, k16, v32, v16, sg = project(xs2, tm_s)
    logf_pad, _ = _forget_call(xn, wf, bf, 1, tm_s)
    logf_s = logf_pad[:, :heads].reshape(db, ds, heads)
    rows = ds * heads
    tiles_per_page = page * heads // LANES
    new_rows = jnp.pad(logf_s.reshape(db, 1, rows), ((0, 0), (0, tiles_per_page - 1), (0, LANES - rows)))
    c_all = _cum_call(page_table, logf_pages, layer, new_rows, heads)
    cq_s = c_all[:, n_pages, 0, :rows].reshape(db, rows, 1)
    pad_rows = ((0, 0), (0, LANES - rows), (0, 0))
    kn = jnp.pad(k16.reshape(db, rows, HEAD_DIM), pad_rows)
    vn = jnp.pad(v16.reshape(db, rows, HEAD_DIM), pad_rows)
    o_s = _attn_sample_call(page_table, cache_k, cache_v, layer, qv.reshape(db, rows, HEAD_DIM), c_all, cq_s,
                            kn, vn, sg.reshape(db, rows, HEAD_DIM), _pick(n_pages, 4))
    ma_s = o_s.reshape(msz, da).astype(BF16)
    hist = state_conv_l
    zeros = jnp.zeros((db, 1, dc), F32)
    h1 = jnp.concatenate([hist[:, 1:2]] + [zeros] * (ds - 1), axis=1).reshape(msz, dc)
    h2 = jnp.concatenate([hist[:, 0:1], hist[:, 1:2]] + [zeros] * (ds - 2), axis=1).reshape(msz, dc)
    mc_s, u_s = _conv_sample_call(xn, w4, w_conv, h1, h2, ds, tc)
    ys = _out_call(ma_s, mc_s, wo, xs2, gpost, _pick(msz, 512), _pick(d, 512)).reshape(db, ds, d)
    conv_s = u_s.reshape(db, ds, dc)[:, ds - (CONV_W - 1):]
    k_s = k32.reshape(db, ds, heads, HEAD_DIM)
    v_s = v32.reshape(db, ds, heads, HEAD_DIM)
    return yp, ys, (k_p, v_p, logf_p, conv_p, k_s, v_s, logf_s, conv_s)


def kernel(x_prompt, x_sample, cache_k, cache_v, cache_logf, state_conv, page_table, g_pre, w_in, b_f, w_conv,
           w_out, g_post):
    depth, n_pool, page, heads = cache_logf.shape
    assert cache_k.shape[-1] == HEAD_DIM and (page * heads) % LANES == 0 and LANES % heads == 0
    assert x_sample.shape[1] >= CONV_W - 1 and x_sample.shape[1] * heads <= LANES
    logf_pages = cache_logf.reshape(depth, n_pool, page * heads // LANES, LANES)
    w_t = jnp.swapaxes(w_in, 1, 2)
    xp, xs = x_prompt, x_sample
    per_layer = []
    for layer in range(depth):
        xp, xs, outs = _layer(xp, xs, cache_k, cache_v, logf_pages, state_conv[layer], page_table, layer,
                              g_pre[layer], w_t, b_f[layer], w_conv[layer], w_out[layer], g_post[layer])
        per_layer.append(outs)
    stacked = [jnp.stack(leaf) for leaf in zip(*per_layer)]
    return (xp, xs, *stacked)
```

```python
import functools

import jax
import jax.numpy as jnp
from jax import lax
from jax.experimental import pallas as pl
from jax.experimental.pallas import tpu as pltpu

HEAD_DIM = 128
CONV_W = 3
EPS = 1e-6
LANES = 128
SUBLANES = 8
V7X_VMEM_BYTES = 64 * 1024 * 1024
VMEM_BUDGET_BYTES = 58 * 1024 * 1024
F32 = jnp.float32
BF16 = jnp.bfloat16
NEG_INF = float("-inf")
LOG2E = 1.4426950408889634
WIDE_KEY_TILES = (4, 2, 1)
RING_SLOTS = 4


def _params(vmem_bytes, n_grid):
    limit = min(int(vmem_bytes * 1.25) + (4 << 20), VMEM_BUDGET_BYTES)
    return pltpu.CompilerParams(dimension_semantics=("arbitrary",) * n_grid, vmem_limit_bytes=limit)


def _silu(z):
    return z * (1.0 / (1.0 + jnp.exp(-z)))


def _split3(x):
    hi = x.astype(BF16)
    r = x - hi.astype(F32)
    mid = r.astype(BF16)
    lo = (r - mid.astype(F32)).astype(BF16)
    return hi, mid, lo


def _dot_f32_by_01(x, w01):
    w = w01.astype(BF16)
    hi, mid, lo = (jnp.dot(p, w, preferred_element_type=F32) for p in _split3(x))
    return (lo + mid) + hi


def _dot_01_by_f32(w01, x):
    w = w01.astype(BF16)
    hi, mid, lo = (jnp.dot(w, p, preferred_element_type=F32) for p in _split3(x))
    return (lo + mid) + hi


def _rmsnorm_kernel(x_ref, g_ref, o_ref):
    x = x_ref[...]
    ms = jnp.mean(x * x, axis=-1, keepdims=True)
    o_ref[...] = (x * lax.rsqrt(ms + EPS) * g_ref[...]).astype(o_ref.dtype)


def _rmsnorm_bf16(x2d, g_row):
    m, d = x2d.shape
    tm = min(256, m)
    return pl.pallas_call(
        _rmsnorm_kernel,
        grid=(m // tm,),
        in_specs=[pl.BlockSpec((tm, d), lambda i: (i, 0)), pl.BlockSpec((1, d), lambda i: (0, 0))],
        out_specs=pl.BlockSpec((tm, d), lambda i: (i, 0)),
        out_shape=jax.ShapeDtypeStruct((m, d), BF16),
        compiler_params=_params(2 * tm * d * 6, 1),
        name="rmsnorm_pre",
    )(x2d, g_row)


def _weights_body(w_ref, o_ref):
    o_ref[...] = w_ref[0].T.astype(BF16)


def _weights_call(w_t, layer, row0, n_rows, out_block_of, name):
    d = w_t.shape[2]
    tr = _pick(n_rows, 256)
    assert row0 % SUBLANES == 0
    return pl.pallas_call(
        _weights_body,
        grid=(n_rows // tr,),
        in_specs=[pl.BlockSpec((pl.Element(1), pl.Element(tr), pl.Element(d)), lambda i: (layer, pl.multiple_of(row0 + i * tr, SUBLANES), 0))],
        out_specs=pl.BlockSpec((d, tr), lambda i: (0, out_block_of(i))),
        out_shape=jax.ShapeDtypeStruct((d, n_rows), BF16),
        compiler_params=_params(2 * tr * d * 6 + tr * d * 8, 1),
        name=name,
    )(w_t)


def _proj_call(body, xn, w, out_dtypes, tm, tn, name, n=None, col0=0):
    m, d = xn.shape
    n = w.shape[1] if n is None else n
    assert col0 % tn == 0 and n % tn == 0
    jb = col0 // tn
    out_bytes = sum(jnp.dtype(t).itemsize for t in out_dtypes)
    vmem = 2 * (tm * d * 2 + d * tn * 2 + tm * tn * out_bytes) + tm * tn * 4
    return pl.pallas_call(
        body,
        grid=(n // tn, m // tm),
        in_specs=[pl.BlockSpec((tm, d), lambda j, i: (i, 0)), pl.BlockSpec((d, tn), lambda j, i: (0, j + jb))],
        out_specs=[pl.BlockSpec((tm, tn), lambda j, i: (i, j)) for _ in out_dtypes],
        out_shape=[jax.ShapeDtypeStruct((m, n), t) for t in out_dtypes],
        compiler_params=_params(vmem, 2),
        name=name,
    )(xn, w)


def _q_body(xn_ref, w_ref, o_ref, *, scale):
    z = jnp.dot(xn_ref[...], w_ref[...], preferred_element_type=F32)
    o_ref[...] = (z * scale).astype(o_ref.dtype)


def _kv_body(xn_ref, w_ref, o32_ref, o16_ref):
    z = jnp.dot(xn_ref[...], w_ref[...], preferred_element_type=F32)
    o32_ref[...] = z
    o16_ref[...] = z.astype(BF16)


def _gate_body(xn_ref, w_ref, o_ref):
    z = jnp.dot(xn_ref[...], w_ref[...], preferred_element_type=F32)
    o_ref[...] = _silu(z).astype(o_ref.dtype)


def _log_sigmoid(z):
    return jnp.minimum(z, 0.0) - jnp.log1p(jnp.exp(-jnp.abs(z)))


def _forget_body(xn_ref, w_ref, b_ref, logf_ref, c_ref, carry_ref, *, tiles_per_seq):
    i = pl.program_id(0)
    tm = xn_ref.shape[0]
    z = lax.dot_general(xn_ref[...], w_ref[...], (((1,), (1,)), ((), ())), preferred_element_type=F32)
    logf = _log_sigmoid(z + b_ref[...])
    logf_ref[...] = logf

    @pl.when(i % tiles_per_seq == 0)
    def _():
        carry_ref[...] = jnp.zeros_like(carry_ref)

    row = lax.broadcasted_iota(jnp.int32, (tm, tm), 0)
    col = lax.broadcasted_iota(jnp.int32, (tm, tm), 1)
    c = _dot_01_by_f32(col <= row, logf) + carry_ref[0:1, :]
    c_ref[...] = c
    carry_ref[...] = jnp.broadcast_to(c[tm - 1:tm, :], carry_ref.shape)


def _forget_call(xn, wf_pad, bf_pad, tiles_per_seq, tm):
    m, d = xn.shape
    vmem = 2 * (tm * d * 2 + d * LANES * 2 + 2 * tm * LANES * 4) + tm * tm * 8
    return pl.pallas_call(
        functools.partial(_forget_body, tiles_per_seq=tiles_per_seq),
        grid=(m // tm,),
        in_specs=[pl.BlockSpec((tm, d), lambda i: (i, 0)),
                  pl.BlockSpec((LANES, d), lambda i: (0, 0)),
                  pl.BlockSpec((1, LANES), lambda i: (0, 0))],
        out_specs=[pl.BlockSpec((tm, LANES), lambda i: (i, 0)), pl.BlockSpec((tm, LANES), lambda i: (i, 0))],
        out_shape=[jax.ShapeDtypeStruct((m, LANES), F32), jax.ShapeDtypeStruct((m, LANES), F32)],
        scratch_shapes=[pltpu.VMEM((SUBLANES, LANES), F32)],
        compiler_params=_params(vmem, 1),
        name="forget_gate",
    )(xn, wf_pad, bf_pad)


def _conv_tail(zb, zg, u, u1, u2, wc_ref, m_ref):
    wc = wc_ref[...]
    y = wc[0:1, :] * u2 + wc[1:2, :] * u1 + wc[2:3, :] * u
    m_ref[...] = (zb * y * _silu(zg)).astype(m_ref.dtype)


def _conv_prompt_body(xn_ref, w_ref, wc_ref, m_ref, cn_ref, carry_ref, *, tiles_per_seq, tc):
    i = pl.program_id(1)
    tm = xn_ref.shape[0]
    z = jnp.dot(xn_ref[...], w_ref[...], preferred_element_type=F32)
    zb, zc, zh, zg = (z[:, k * tc:(k + 1) * tc] for k in range(4))
    u = zc * zh

    @pl.when(i % tiles_per_seq == 0)
    def _():
        carry_ref[...] = jnp.zeros_like(carry_ref)

    prev = carry_ref[...]
    p1 = prev[SUBLANES - 1:SUBLANES, :]
    p2 = prev[SUBLANES - 2:SUBLANES - 1, :]
    row = lax.broadcasted_iota(jnp.int32, (tm, tc), 0)
    u1 = jnp.where(row >= 1, pltpu.roll(u, 1, axis=0), p1)
    u2 = jnp.where(row >= 2, pltpu.roll(u, 2, axis=0), jnp.where(row == 1, p1, p2))
    carry_ref[...] = u[tm - SUBLANES:tm, :]
    cn_ref[0] = u[tm - (CONV_W - 1):tm, :]
    _conv_tail(zb, zg, u, u1, u2, wc_ref, m_ref)


def _conv_sample_body(xn_ref, w_ref, wc_ref, h1_ref, h2_ref, m_ref, u_ref, *, dec_seq, tc):
    tm = xn_ref.shape[0]
    z = jnp.dot(xn_ref[...], w_ref[...], preferred_element_type=F32)
    zb, zc, zh, zg = (z[:, k * tc:(k + 1) * tc] for k in range(4))
    u = zc * zh
    u_ref[...] = u
    t = lax.rem(lax.broadcasted_iota(jnp.int32, (tm, tc), 0), dec_seq)
    u1 = jnp.where(t >= 1, pltpu.roll(u, 1, axis=0), h1_ref[...])
    u2 = jnp.where(t >= 2, pltpu.roll(u, 2, axis=0), h2_ref[...])
    _conv_tail(zb, zg, u, u1, u2, wc_ref, m_ref)


def _conv_prompt_call(xn, w4, wconv, batch, tm, tc):
    m, d = xn.shape
    dc = wconv.shape[1]
    tiles_per_seq = m // batch // tm
    vmem = 2 * (tm * d * 2 + d * 4 * tc * 2 + tm * tc * 2) + tm * 4 * tc * 4 * 2
    return pl.pallas_call(
        functools.partial(_conv_prompt_body, tiles_per_seq=tiles_per_seq, tc=tc),
        grid=(dc // tc, m // tm),
        in_specs=[pl.BlockSpec((tm, d), lambda j, i: (i, 0)),
                  pl.BlockSpec((d, 4 * tc), lambda j, i: (0, j)),
                  pl.BlockSpec((CONV_W, tc), lambda j, i: (0, j))],
        out_specs=[pl.BlockSpec((tm, tc), lambda j, i: (i, j)),
                   pl.BlockSpec((1, CONV_W - 1, tc), lambda j, i: (i // tiles_per_seq, 0, j))],
        out_shape=[jax.ShapeDtypeStruct((m, dc), BF16), jax.ShapeDtypeStruct((batch, CONV_W - 1, dc), F32)],
        scratch_shapes=[pltpu.VMEM((SUBLANES, tc), F32)],
        compiler_params=_params(vmem, 2),
        name="conv_prompt",
    )(xn, w4, wconv)


def _conv_sample_call(xn, w4, wconv, h1, h2, dec_seq, tc):
    m, d = xn.shape
    dc = wconv.shape[1]
    vmem = 2 * (m * d * 2 + d * 4 * tc * 2 + m * tc * 14) + m * 4 * tc * 4 * 2
    return pl.pallas_call(
        functools.partial(_conv_sample_body, dec_seq=dec_seq, tc=tc),
        grid=(dc // tc,),
        in_specs=[pl.BlockSpec((m, d), lambda j: (0, 0)),
                  pl.BlockSpec((d, 4 * tc), lambda j: (0, j)),
                  pl.BlockSpec((CONV_W, tc), lambda j: (0, j)),
                  pl.BlockSpec((m, tc), lambda j: (0, j)),
                  pl.BlockSpec((m, tc), lambda j: (0, j))],
        out_specs=[pl.BlockSpec((m, tc), lambda j: (0, j)), pl.BlockSpec((m, tc), lambda j: (0, j))],
        out_shape=[jax.ShapeDtypeStruct((m, dc), BF16), jax.ShapeDtypeStruct((m, dc), F32)],
        compiler_params=_params(vmem, 1),
        name="conv_sample",
    )(xn, w4, wconv, h1, h2)


def _out_body(ma_ref, mc_ref, wa_ref, wc_ref, x_ref, g_ref, o_ref, *, tn, n_tiles):
    j = pl.program_id(1)
    z = (jnp.dot(ma_ref[...], wa_ref[...], preferred_element_type=F32)
         + jnp.dot(mc_ref[...], wc_ref[...], preferred_element_type=F32))
    o_ref[:, pl.ds(pl.multiple_of(j * tn, tn), tn)] = z

    @pl.when(j == n_tiles - 1)
    def _():
        zz = o_ref[...]
        ms = jnp.mean(zz * zz, axis=-1, keepdims=True)
        o_ref[...] = x_ref[...] + zz * lax.rsqrt(ms + EPS) * g_ref[...]


def _out_call(ma, mc, wo, x2d, g_row, tm, tn):
    m, da = ma.shape
    dcv = mc.shape[1]
    d = x2d.shape[1]
    assert da == dcv, "w_out row blocks are indexed in units of the attention width"
    n_tiles = d // tn
    vmem = 2 * (2 * tm * d * 4 + tm * (da + dcv) * 2 + (da + dcv) * tn * 2) + tm * tn * 4
    return pl.pallas_call(
        functools.partial(_out_body, tn=tn, n_tiles=n_tiles),
        grid=(m // tm, n_tiles),
        in_specs=[pl.BlockSpec((tm, da), lambda i, j: (i, 0)),
                  pl.BlockSpec((tm, dcv), lambda i, j: (i, 0)),
                  pl.BlockSpec((da, tn), lambda i, j: (0, j)),
                  pl.BlockSpec((dcv, tn), lambda i, j: (1, j)),
                  pl.BlockSpec((tm, d), lambda i, j: (i, 0)),
                  pl.BlockSpec((1, d), lambda i, j: (0, 0))],
        out_specs=pl.BlockSpec((tm, d), lambda i, j: (i, 0)),
        out_shape=jax.ShapeDtypeStruct((m, d), F32),
        compiler_params=_params(vmem, 2),
        name="out_proj",
    )(ma, mc, wo, wo, x2d, g_row)


def _attn_prompt_body(q_ref, k_ref, v_ref, c_ref, sg_ref, o_ref, vt_ref, crep_ref, *, blk, qsub):
    qi = pl.program_id(2)
    seq = k_ref.shape[1]

    @pl.when(qi == 0)
    def _():
        for c0 in range(0, seq, blk):
            vt_ref[:, c0:c0 + blk] = v_ref[0, c0:c0 + blk, :].astype(F32).T.astype(BF16)
            row = c_ref[0, 0, :, c0:c0 + blk] * LOG2E
            crep_ref[c0:c0 + blk, :] = jnp.broadcast_to(row, (LANES, blk)).T

    n_sub = blk // qsub
    q_subs = [q_ref[0, j * qsub:(j + 1) * qsub, :].astype(F32).T.astype(BF16) for j in range(n_sub)]
    c_q = c_ref[0, 0, :, pl.ds(pl.multiple_of(qi * blk, blk), blk)] * LOG2E
    cq_subs = [c_q[:, j * qsub:(j + 1) * qsub] for j in range(n_sub)]

    def tile(first, width, carry, diagonal):
        ks = pl.multiple_of(first * width, width)
        k = k_ref[0, pl.ds(ks, width), :]
        vt = vt_ref[:, pl.ds(ks, width)]
        c_k = jnp.tile(crep_ref[pl.ds(ks, width), :], (1, qsub // LANES))
        out = []
        for j, (m, l, acc) in enumerate(carry):
            st = jnp.dot(k, q_subs[j], preferred_element_type=F32) - c_k
            if diagonal:
                key = lax.broadcasted_iota(jnp.int32, (width, qsub), 0)
                qry = lax.broadcasted_iota(jnp.int32, (blk, qsub), 1) + j * qsub
                st = jnp.where(key <= qry, st, NEG_INF)
            m_new = jnp.maximum(m, jnp.max(st, axis=0, keepdims=True) + cq_subs[j])
            alpha = jnp.exp2(m - m_new)
            pt = jnp.exp2(st - (m_new - cq_subs[j]))
            l = alpha * l + jnp.sum(pt, axis=0, keepdims=True)
            acc = alpha * acc + jnp.dot(vt, pt.astype(BF16), preferred_element_type=F32)
            out.append((m_new, l, acc))
        return tuple(out)

    init = tuple((jnp.full((1, qsub), NEG_INF, F32), jnp.zeros((1, qsub), F32), jnp.zeros((HEAD_DIM, qsub), F32))
                 for _ in range(n_sub))
    carry, done = init, 0
    for tiles in WIDE_KEY_TILES:
        n = (qi - done) // tiles
        carry = lax.fori_loop(done // tiles, done // tiles + n,
                              functools.partial(lambda p, c, w: tile(p, w, c, False), w=tiles * blk), carry)
        done = done + n * tiles
    for j, (_, l, acc) in enumerate(tile(qi, blk, carry, True)):
        o_ref[0, j * qsub:(j + 1) * qsub, :] = (
            (acc * (1.0 / l)).T * sg_ref[0, j * qsub:(j + 1) * qsub, :].astype(F32)).astype(o_ref.dtype)


def _attn_prompt_call(q, k, v, c_row, sg, blk):
    b, s, da = q.shape
    h = da // HEAD_DIM
    vmem = (2 * (2 * s * HEAD_DIM * 2 + 3 * blk * HEAD_DIM * 2 + s * 4 * SUBLANES)
            + s * HEAD_DIM * 2 + s * LANES * 4 + 6 * blk * blk * 4)
    return pl.pallas_call(
        functools.partial(_attn_prompt_body, blk=blk, qsub=blk),
        grid=(b, h, s // blk),
        in_specs=[pl.BlockSpec((1, blk, HEAD_DIM), lambda bi, hi, qi: (bi, qi, hi)),
                  pl.BlockSpec((1, s, HEAD_DIM), lambda bi, hi, qi: (bi, 0, hi)),
                  pl.BlockSpec((1, s, HEAD_DIM), lambda bi, hi, qi: (bi, 0, hi)),
                  pl.BlockSpec((1, 1, 1, s), lambda bi, hi, qi: (bi, hi, 0, 0)),
                  pl.BlockSpec((1, blk, HEAD_DIM), lambda bi, hi, qi: (bi, qi, hi))],
        out_specs=pl.BlockSpec((1, blk, HEAD_DIM), lambda bi, hi, qi: (bi, qi, hi)),
        out_shape=jax.ShapeDtypeStruct((b, s, da), BF16),
        scratch_shapes=[pltpu.VMEM((HEAD_DIM, s), BF16), pltpu.VMEM((s, LANES), F32)],
        compiler_params=_params(vmem, 3),
        name="attn_prompt",
    )(q, k, v, c_row, sg)


def _cum_body(pt_ref, *refs, n_pages, heads):
    del pt_ref
    x_refs, c_ref = refs[:n_pages + 1], refs[n_pages + 1]
    rows = x_refs[0].shape[-2]
    n_valid = (n_pages + 1) * rows
    n = -(-n_valid // LANES) * LANES
    blocks = [r[...] for r in x_refs]
    if n > n_valid:
        blocks.append(jnp.zeros((n - n_valid, LANES), F32))
    x = jnp.concatenate(blocks, axis=0)
    l1 = lax.broadcasted_iota(jnp.int32, (LANES, LANES), 0)
    l2 = lax.broadcasted_iota(jnp.int32, (LANES, LANES), 1)
    same_head = (l1 % heads) == (l2 % heads)
    within = same_head & ((l1 // heads) <= (l2 // heads))
    r1 = lax.broadcasted_iota(jnp.int32, (n, n), 0)
    r2 = lax.broadcasted_iota(jnp.int32, (n, n), 1)
    c = _dot_f32_by_01(x, within) + _dot_01_by_f32(r2 < r1, _dot_f32_by_01(x, same_head))
    c_ref[0] = c[:n_valid].reshape(n_pages + 1, rows, LANES)


def _cum_call(page_table, logf_pages, layer, new_rows, heads):
    db, n_pages = page_table.shape
    rows = logf_pages.shape[2]

    def page_map(p):
        return lambda s, pt: (layer, pt[s, p], 0, 0)

    in_specs = [pl.BlockSpec((None, None, rows, LANES), page_map(p)) for p in range(n_pages)]
    in_specs.append(pl.BlockSpec((None, rows, LANES), lambda s, pt: (s, 0, 0)))
    n = (n_pages + 1) * rows
    return pl.pallas_call(
        functools.partial(_cum_body, n_pages=n_pages, heads=heads),
        grid_spec=pltpu.PrefetchScalarGridSpec(
            num_scalar_prefetch=1, grid=(db,), in_specs=in_specs,
            out_specs=pl.BlockSpec((1, n_pages + 1, rows, LANES), lambda s, pt: (s, 0, 0, 0))),
        out_shape=jax.ShapeDtypeStruct((db, n_pages + 1, rows, LANES), F32),
        compiler_params=_params(4 * n * LANES * 4 + n * n * 8, 1),
        name="cache_logf_cumsum",
    )(page_table, *([logf_pages] * n_pages), new_rows)


def _attn_sample_body(pt_ref, kc_ref, vc_ref, q_ref, c_ref, cn_ref, cq_ref, kn_ref, vn_ref, sg_ref, o_ref,
                      kbuf, vbuf, sem, m_ref, l_ref, acc_ref, *, layer, group, heads, dec_seq, n_groups, n_slots):
    g = pl.program_id(1)
    step = pl.program_id(0) * n_groups + g
    n_steps = pl.num_programs(0) * n_groups
    rows = heads * dec_seq
    q = q_ref[0]
    cq = cq_ref[0] * LOG2E

    def page_copies(t, slot):
        seq = t // n_groups
        first = (t - seq * n_groups) * group
        copies = []
        for i in range(group):
            page = pt_ref[seq, first + i]
            copies.append(pltpu.make_async_copy(kc_ref.at[layer, page], kbuf.at[slot, i], sem.at[0, slot]))
            copies.append(pltpu.make_async_copy(vc_ref.at[layer, page], vbuf.at[slot, i], sem.at[1, slot]))
        return copies

    @pl.when(step == 0)
    def _():
        for t in range(n_slots - 1):
            for cp in page_copies(t, t):
                cp.start()

    ahead = step + (n_slots - 1)

    @pl.when(ahead < n_steps)
    def _():
        for cp in page_copies(ahead, ahead % n_slots):
            cp.start()

    slot = step % n_slots
    for cp in page_copies(step, slot):
        cp.wait()

    @pl.when(g == 0)
    def _():
        m_ref[...] = jnp.full(m_ref.shape, NEG_INF, F32)
        l_ref[...] = jnp.zeros_like(l_ref)
        acc_ref[...] = jnp.zeros_like(acc_ref)

    r_head = lax.broadcasted_iota(jnp.int32, (rows, LANES), 0) % heads
    lane = lax.broadcasted_iota(jnp.int32, (rows, LANES), 1)
    same_head = (lane % heads) == r_head

    def local(kf, vf, bias_tiles):
        s = lax.dot_general(q, kf, (((1,), (1,)), ((), ())), preferred_element_type=F32)
        s = jnp.concatenate([s[:, t * LANES:(t + 1) * LANES] + bias_tiles[t] for t in range(len(bias_tiles))], axis=1)
        m_blk = jnp.max(s, axis=-1, keepdims=True)
        p = jnp.exp2(s - m_blk)
        return m_blk, jnp.sum(p, axis=-1, keepdims=True), jnp.dot(p.astype(BF16), vf, preferred_element_type=F32)

    def merge(parts):
        m, l, acc = m_ref[...], l_ref[...], acc_ref[...]
        for m_blk, l_blk, o_blk in parts:
            m_new = jnp.maximum(m, m_blk + cq)
            a_old, a_blk = jnp.exp2(m - m_new), jnp.exp2(m_blk + cq - m_new)
            l = a_old * l + a_blk * l_blk
            acc = a_old * acc + a_blk * o_blk
            m = m_new
        m_ref[...], l_ref[...], acc_ref[...] = m, l, acc

    head_mask = jnp.where(same_head, 0.0, NEG_INF).astype(F32)
    tiles_per_page = kbuf.shape[2] * heads // LANES
    parts = []
    for i in range(group):
        kf = kbuf[slot, i].reshape(-1, HEAD_DIM).astype(BF16)
        vf = vbuf[slot, i].reshape(-1, HEAD_DIM).astype(BF16)
        c = c_ref[0, i] * LOG2E
        parts.append(local(kf, vf, [head_mask - c[t:t + 1, :] for t in range(tiles_per_page)]))
    merge(parts)

    @pl.when(g == n_groups - 1)
    def _():
        r_query = lax.broadcasted_iota(jnp.int32, (rows, LANES), 0) // heads
        ok = same_head & ((lane // heads) <= r_query) & (lane < rows)
        bias = jnp.where(ok, 0.0, NEG_INF).astype(F32) - cn_ref[0, 0][0:1, :] * LOG2E
        merge([local(kn_ref[0], vn_ref[0], [bias])])
        o_ref[0] = acc_ref[...] * (1.0 / l_ref[...]) * sg_ref[0].astype(F32)


def _attn_sample_call(page_table, cache_k, cache_v, layer, q_rows, c_all, cq_col, kn_rows, vn_rows, sg_rows,
                      group):
    db, n_pages = page_table.shape
    _, _, page, heads, hd = cache_k.shape
    rows = q_rows.shape[1]
    dec_seq = rows // heads
    n_groups = n_pages // group
    tiles_per_page = page * heads // LANES

    n_slots = RING_SLOTS
    assert db * n_groups >= n_slots - 1
    in_specs = [
        pl.BlockSpec(memory_space=pl.ANY),
        pl.BlockSpec(memory_space=pl.ANY),
        pl.BlockSpec((1, rows, hd), lambda s, g, pt: (s, 0, 0)),
        pl.BlockSpec((1, group, tiles_per_page, LANES), lambda s, g, pt: (s, g, 0, 0)),
        pl.BlockSpec((1, 1, tiles_per_page, LANES), lambda s, g, pt: (s, n_pages, 0, 0)),
        pl.BlockSpec((1, rows, 1), lambda s, g, pt: (s, 0, 0)),
        pl.BlockSpec((1, LANES, hd), lambda s, g, pt: (s, 0, 0)),
        pl.BlockSpec((1, LANES, hd), lambda s, g, pt: (s, 0, 0)),
        pl.BlockSpec((1, rows, hd), lambda s, g, pt: (s, 0, 0)),
    ]
    ring = (n_slots, group, page, heads, hd)
    vmem = 2 * n_slots * group * page * heads * hd * 4 + 4 * rows * page * heads * 4 + (4 << 20)
    return pl.pallas_call(
        functools.partial(_attn_sample_body, layer=layer, group=group, heads=heads, dec_seq=dec_seq,
                          n_groups=n_groups, n_slots=n_slots),
        grid_spec=pltpu.PrefetchScalarGridSpec(
            num_scalar_prefetch=1, grid=(db, n_groups), in_specs=in_specs,
            out_specs=pl.BlockSpec((1, rows, hd), lambda s, g, pt: (s, 0, 0)),
            scratch_shapes=[pltpu.VMEM(ring, F32), pltpu.VMEM(ring, F32), pltpu.SemaphoreType.DMA((2, n_slots)),
                            pltpu.VMEM((rows, 1), F32), pltpu.VMEM((rows, 1), F32), pltpu.VMEM((rows, hd), F32)]),
        out_shape=jax.ShapeDtypeStruct((db, rows, hd), F32),
        compiler_params=_params(vmem, 2),
        name="attn_sample",
    )(page_table, cache_k, cache_v, q_rows, c_all, c_all, cq_col, kn_rows, vn_rows, sg_rows)


def _pick(n, pref):
    t = min(pref, n)
    assert n % t == 0, (n, t)
    return t


def _layer(xp, xs, cache_k, cache_v, logf_pages, state_conv_l, page_table, layer,
           g_pre, w_t, b_f, w_conv, w_out, g_post):
    b, s, d = xp.shape
    db, ds, _ = xs.shape
    heads = b_f.shape[0]
    da = heads * HEAD_DIM
    dc = w_conv.shape[1]
    page = cache_k.shape[2]
    n_pages = page_table.shape[1]
    mp, msz = b * s, db * ds
    scale = HEAD_DIM ** -0.5 * LOG2E

    tc = _pick(dc, 256)
    n_ct = dc // tc
    off = 3 * da + heads
    assert w_t.shape[1] == off + da + 4 * dc and _pick(4 * dc, 256) == tc
    wqkv = _weights_call(w_t, layer, 0, 3 * da, lambda i: i, "weights_qkv")
    wga = _weights_call(w_t, layer, off, da, lambda i: i, "weights_gate")
    w4 = _weights_call(w_t, layer, off + da, 4 * dc, lambda i: 4 * (i % n_ct) + i // n_ct, "weights_conv")
    wf = jnp.pad(w_t[layer, 3 * da:off], ((0, LANES - heads), (0, 0))).astype(BF16)
    bf = jnp.pad(b_f, (0, LANES - heads)).reshape(1, LANES).astype(F32)
    wo = w_out.astype(BF16)
    gpre = g_pre.reshape(1, d)
    gpost = g_post.reshape(1, d)

    def project(x2d, tm):
        xn = _rmsnorm_bf16(x2d, gpre)
        tn = _pick(da, 1024)
        (qv,) = _proj_call(functools.partial(_q_body, scale=scale), xn, wqkv, [BF16], tm, tn, "proj_q", da, 0)
        k32, k16 = _proj_call(_kv_body, xn, wqkv, [F32, BF16], tm, tn, "proj_k", da, da)
        v32, v16 = _proj_call(_kv_body, xn, wqkv, [F32, BF16], tm, tn, "proj_v", da, 2 * da)
        (sg,) = _proj_call(_gate_body, xn, wga, [BF16], tm, tn, "proj_gate")
        return xn, qv, k32, k16, v32, v16, sg

    xp2 = xp.reshape(mp, d)
    tm_p = _pick(s, 512)
    xn, qv, k32, k16, v32, v16, sg = project(xp2, tm_p)
    tm_f = _pick(s, 256)
    logf_pad, c_pad = _forget_call(xn, wf, bf, s // tm_f, tm_f)
    logf_p = logf_pad[:, :heads].reshape(b, s, heads)
    c_p = c_pad[:, :heads].reshape(b, s, heads)
    c_row = c_p.transpose(0, 2, 1).reshape(b, heads, 1, s)
    blk = _pick(s, 512)
    ma_p = _attn_prompt_call(qv.reshape(b, s, da), k16.reshape(b, s, da), v16.reshape(b, s, da),
                             c_row, sg.reshape(b, s, da), blk)
    mc_p, conv_p = _conv_prompt_call(xn, w4, w_conv, b, tm_p, tc)
    tm_o = _pick(s, 512)
    yp = _out_call(ma_p.reshape(mp, da), mc_p, wo, xp2, gpost, tm_o, _pick(d, 512)).reshape(b, s, d)
    k_p = k32.reshape(b, s, heads, HEAD_DIM)
    v_p = v32.reshape(b, s, heads, HEAD_DIM)

    xs2 = xs.reshape(msz, d)
    tm_s = _pick(msz, 512)
    xn, qv, k32, k16, v32, v16, sg = project(xs2, tm_s)
    logf_pad, _ = _forget_call(xn, wf, bf, 1, tm_s)
    logf_s = logf_pad[:, :heads].reshape(db, ds, heads)
    rows = ds * heads
    tiles_per_page = page * heads // LANES
    new_rows = jnp.pad(logf_s.reshape(db, 1, rows), ((0, 0), (0, tiles_per_page - 1), (0, LANES - rows)))
    c_all = _cum_call(page_table, logf_pages, layer, new_rows, heads)
    cq_s = c_all[:, n_pages, 0, :rows].reshape(db, rows, 1)
    pad_rows = ((0, 0), (0, LANES - rows), (0, 0))
    kn = jnp.pad(k16.reshape(db, rows, HEAD_DIM), pad_rows)
    vn = jnp.pad(v16.reshape(db, rows, HEAD_DIM), pad_rows)
    o_s = _attn_sample_call(page_table, cache_k, cache_v, layer, qv.reshape(db, rows, HEAD_DIM), c_all, cq_s,
                            kn, vn, sg.reshape(db, rows, HEAD_DIM), _pick(n_pages, 4))
    ma_s = o_s.reshape(msz, da).astype(BF16)
    hist = state_conv_l
    zeros = jnp.zeros((db, 1, dc), F32)
    h1 = jnp.concatenate([hist[:, 1:2]] + [zeros] * (ds - 1), axis=1).reshape(msz, dc)
    h2 = jnp.concatenate([hist[:, 0:1], hist[:, 1:2]] + [zeros] * (ds - 2), axis=1).reshape(msz, dc)
    mc_s, u_s = _conv_sample_call(xn, w4, w_conv, h1, h2, ds, tc)
    ys = _out_call(ma_s, mc_s, wo, xs2, gpost, _pick(msz, 512), _pick(d, 512)).reshape(db, ds, d)
    conv_s = u_s.reshape(db, ds, dc)[:, ds - (CONV_W - 1):]
    k_s = k32.reshape(db, ds, heads, HEAD_DIM)
    v_s = v32.reshape(db, ds, heads, HEAD_DIM)
    return yp, ys, (k_p, v_p, logf_p, conv_p, k_s, v_s, logf_s, conv_s)


def kernel(x_prompt, x_sample, cache_k, cache_v, cache_logf, state_conv, page_table, g_pre, w_in, b_f, w_conv,
           w_out, g_post):
    depth, n_pool, page, heads = cache_logf.shape
    assert cache_k.shape[-1] == HEAD_DIM and (page * heads) % LANES == 0 and LANES % heads == 0
    assert x_sample.shape[1] >= CONV_W - 1 and x_sample.shape[1] * heads <= LANES
    logf_pages = cache_logf.reshape(depth, n_pool, page * heads // LANES, LANES)
    w_t = jnp.swapaxes(w_in, 1, 2)
    xp, xs = x_prompt, x_sample
    per_layer = []
    for layer in range(depth):
        xp, xs, outs = _layer(xp, xs, cache_k, cache_v, logf_pages, state_conv[layer], page_table, layer,
                              g_pre[layer], w_t, b_f[layer], w_conv[layer], w_out[layer], g_post[layer])
        per_layer.append(outs)
    stacked = [jnp.stack(leaf) for leaf in zip(*per_layer)]
    return (xp, xs, *stacked)
```

```python
import functools

import jax
import jax.numpy as jnp
from jax import lax
from jax.experimental import pallas as pl
from jax.experimental.pallas import tpu as pltpu

HEAD_DIM = 128
CONV_W = 3
EPS = 1e-6
LANES = 128
SUBLANES = 8
V7X_VMEM_BYTES = 64 * 1024 * 1024
VMEM_BUDGET_BYTES = 58 * 1024 * 1024
F32 = jnp.float32
BF16 = jnp.bfloat16
NEG_INF = float("-inf")
LOG2E = 1.4426950408889634
WIDE_KEY_TILES = (4, 2, 1)
RING_SLOTS = 4


def _params(vmem_bytes, n_grid):
    limit = min(int(vmem_bytes * 1.25) + (4 << 20), VMEM_BUDGET_BYTES)
    return pltpu.CompilerParams(dimension_semantics=("arbitrary",) * n_grid, vmem_limit_bytes=limit)


def _silu(z):
    return z * (1.0 / (1.0 + jnp.exp(-z)))


def _split3(x):
    hi = x.astype(BF16)
    r = x - hi.astype(F32)
    mid = r.astype(BF16)
    lo = (r - mid.astype(F32)).astype(BF16)
    return hi, mid, lo


def _dot_f32_by_01(x, w01):
    w = w01.astype(BF16)
    hi, mid, lo = (jnp.dot(p, w, preferred_element_type=F32) for p in _split3(x))
    return (lo + mid) + hi


def _dot_01_by_f32(w01, x):
    w = w01.astype(BF16)
    hi, mid, lo = (jnp.dot(w, p, preferred_element_type=F32) for p in _split3(x))
    return (lo + mid) + hi


def _rmsnorm_kernel(x_ref, g_ref, o_ref):
    x = x_ref[...]
    ms = jnp.mean(x * x, axis=-1, keepdims=True)
    o_ref[...] = (x * lax.rsqrt(ms + EPS) * g_ref[...]).astype(o_ref.dtype)


def _rmsnorm_bf16(x2d, g_row):
    m, d = x2d.shape
    tm = min(256, m)
    return pl.pallas_call(
        _rmsnorm_kernel,
        grid=(m // tm,),
        in_specs=[pl.BlockSpec((tm, d), lambda i: (i, 0)), pl.BlockSpec((1, d), lambda i: (0, 0))],
        out_specs=pl.BlockSpec((tm, d), lambda i: (i, 0)),
        out_shape=jax.ShapeDtypeStruct((m, d), BF16),
        compiler_params=_params(2 * tm * d * 6, 1),
        name="rmsnorm_pre",
    )(x2d, g_row)


def _weights_body(w_ref, o_ref):
    o_ref[...] = w_ref[0].T.astype(BF16)


def _weights_call(w_t, layer, row0, n_rows, out_block_of, name):
    d = w_t.shape[2]
    tr = _pick(n_rows, 256)
    assert row0 % SUBLANES == 0
    return pl.pallas_call(
        _weights_body,
        grid=(n_rows // tr,),
        in_specs=[pl.BlockSpec((pl.Element(1), pl.Element(tr), pl.Element(d)), lambda i: (layer, pl.multiple_of(row0 + i * tr, SUBLANES), 0))],
        out_specs=pl.BlockSpec((d, tr), lambda i: (0, out_block_of(i))),
        out_shape=jax.ShapeDtypeStruct((d, n_rows), BF16),
        compiler_params=_params(2 * tr * d * 6 + tr * d * 8, 1),
        name=name,
    )(w_t)


def _proj_call(body, xn, w, out_dtypes, tm, tn, name, n=None, col0=0):
    m, d = xn.shape
    n = w.shape[1] if n is None else n
    assert col0 % tn == 0 and n % tn == 0
    jb = col0 // tn
    out_bytes = sum(jnp.dtype(t).itemsize for t in out_dtypes)
    vmem = 2 * (tm * d * 2 + d * tn * 2 + tm * tn * out_bytes) + tm * tn * 4
    return pl.pallas_call(
        body,
        grid=(n // tn, m // tm),
        in_specs=[pl.BlockSpec((tm, d), lambda j, i: (i, 0)), pl.BlockSpec((d, tn), lambda j, i: (0, j + jb))],
        out_specs=[pl.BlockSpec((tm, tn), lambda j, i: (i, j)) for _ in out_dtypes],
        out_shape=[jax.ShapeDtypeStruct((m, n), t) for t in out_dtypes],
        compiler_params=_params(vmem, 2),
        name=name,
    )(xn, w)


def _q_body(xn_ref, w_ref, o_ref, *, scale):
    z = jnp.dot(xn_ref[...], w_ref[...], preferred_element_type=F32)
    o_ref[...] = (z * scale).astype(o_ref.dtype)


def _kv_body(xn_ref, w_ref, o32_ref, o16_ref):
    z = jnp.dot(xn_ref[...], w_ref[...], preferred_element_type=F32)
    o32_ref[...] = z
    o16_ref[...] = z.astype(BF16)


def _gate_body(xn_ref, w_ref, o_ref):
    z = jnp.dot(xn_ref[...], w_ref[...], preferred_element_type=F32)
    o_ref[...] = _silu(z).astype(o_ref.dtype)


def _log_sigmoid(z):
    return jnp.minimum(z, 0.0) - jnp.log1p(jnp.exp(-jnp.abs(z)))


def _forget_body(xn_ref, w_ref, b_ref, logf_ref, c_ref, carry_ref, *, tiles_per_seq):
    i = pl.program_id(0)
    tm = xn_ref.shape[0]
    z = lax.dot_general(xn_ref[...], w_ref[...], (((1,), (1,)), ((), ())), preferred_element_type=F32)
    logf = _log_sigmoid(z + b_ref[...])
    logf_ref[...] = logf

    @pl.when(i % tiles_per_seq == 0)
    def _():
        carry_ref[...] = jnp.zeros_like(carry_ref)

    row = lax.broadcasted_iota(jnp.int32, (tm, tm), 0)
    col = lax.broadcasted_iota(jnp.int32, (tm, tm), 1)
    c = _dot_01_by_f32(col <= row, logf) + carry_ref[0:1, :]
    c_ref[...] = c
    carry_ref[...] = jnp.broadcast_to(c[tm - 1:tm, :], carry_ref.shape)


def _forget_call(xn, wf_pad, bf_pad, tiles_per_seq, tm):
    m, d = xn.shape
    vmem = 2 * (tm * d * 2 + d * LANES * 2 + 2 * tm * LANES * 4) + tm * tm * 8
    return pl.pallas_call(
        functools.partial(_forget_body, tiles_per_seq=tiles_per_seq),
        grid=(m // tm,),
        in_specs=[pl.BlockSpec((tm, d), lambda i: (i, 0)),
                  pl.BlockSpec((LANES, d), lambda i: (0, 0)),
                  pl.BlockSpec((1, LANES), lambda i: (0, 0))],
        out_specs=[pl.BlockSpec((tm, LANES), lambda i: (i, 0)), pl.BlockSpec((tm, LANES), lambda i: (i, 0))],
        out_shape=[jax.ShapeDtypeStruct((m, LANES), F32), jax.ShapeDtypeStruct((m, LANES), F32)],
        scratch_shapes=[pltpu.VMEM((SUBLANES, LANES), F32)],
        compiler_params=_params(vmem, 1),
        name="forget_gate",
    )(xn, wf_pad, bf_pad)


def _conv_tail(zb, zg, u, u1, u2, wc_ref, m_ref):
    wc = wc_ref[...]
    y = wc[0:1, :] * u2 + wc[1:2, :] * u1 + wc[2:3, :] * u
    m_ref[...] = (zb * y * _silu(zg)).astype(m_ref.dtype)


def _conv_prompt_body(xn_ref, w_ref, wc_ref, m_ref, cn_ref, carry_ref, *, tiles_per_seq, tc):
    i = pl.program_id(1)
    tm = xn_ref.shape[0]
    z = jnp.dot(xn_ref[...], w_ref[...], preferred_element_type=F32)
    zb, zc, zh, zg = (z[:, k * tc:(k + 1) * tc] for k in range(4))
    u = zc * zh

    @pl.when(i % tiles_per_seq == 0)
    def _():
        carry_ref[...] = jnp.zeros_like(carry_ref)

    prev = carry_ref[...]
    p1 = prev[SUBLANES - 1:SUBLANES, :]
    p2 = prev[SUBLANES - 2:SUBLANES - 1, :]
    row = lax.broadcasted_iota(jnp.int32, (tm, tc), 0)
    u1 = jnp.where(row >= 1, pltpu.roll(u, 1, axis=0), p1)
    u2 = jnp.where(row >= 2, pltpu.roll(u, 2, axis=0), jnp.where(row == 1, p1, p2))
    carry_ref[...] = u[tm - SUBLANES:tm, :]
    cn_ref[0] = u[tm - (CONV_W - 1):tm, :]
    _conv_tail(zb, zg, u, u1, u2, wc_ref, m_ref)


def _conv_sample_body(xn_ref, w_ref, wc_ref, h1_ref, h2_ref, m_ref, u_ref, *, dec_seq, tc):
    tm = xn_ref.shape[0]
    z = jnp.dot(xn_ref[...], w_ref[...], preferred_element_type=F32)
    zb, zc, zh, zg = (z[:, k * tc:(k + 1) * tc] for k in range(4))
    u = zc * zh
    u_ref[...] = u
    t = lax.rem(lax.broadcasted_iota(jnp.int32, (tm, tc), 0), dec_seq)
    u1 = jnp.where(t >= 1, pltpu.roll(u, 1, axis=0), h1_ref[...])
    u2 = jnp.where(t >= 2, pltpu.roll(u, 2, axis=0), h2_ref[...])
    _conv_tail(zb, zg, u, u1, u2, wc_ref, m_ref)


def _conv_prompt_call(xn, w4, wconv, batch, tm, tc):
    m, d = xn.shape
    dc = wconv.shape[1]
    tiles_per_seq = m // batch // tm
    vmem = 2 * (tm * d * 2 + d * 4 * tc * 2 + tm * tc * 2) + tm * 4 * tc * 4 * 2
    return pl.pallas_call(
        functools.partial(_conv_prompt_body, tiles_per_seq=tiles_per_seq, tc=tc),
        grid=(dc // tc, m // tm),
        in_specs=[pl.BlockSpec((tm, d), lambda j, i: (i, 0)),
                  pl.BlockSpec((d, 4 * tc), lambda j, i: (0, j)),
                  pl.BlockSpec((CONV_W, tc), lambda j, i: (0, j))],
        out_specs=[pl.BlockSpec((tm, tc), lambda j, i: (i, j)),
                   pl.BlockSpec((1, CONV_W - 1, tc), lambda j, i: (i // tiles_per_seq, 0, j))],
        out_shape=[jax.ShapeDtypeStruct((m, dc), BF16), jax.ShapeDtypeStruct((batch, CONV_W - 1, dc), F32)],
        scratch_shapes=[pltpu.VMEM((SUBLANES, tc), F32)],
        compiler_params=_params(vmem, 2),
        name="conv_prompt",
    )(xn, w4, wconv)


def _conv_sample_call(xn, w4, wconv, h1, h2, dec_seq, tc):
    m, d = xn.shape
    dc = wconv.shape[1]
    vmem = 2 * (m * d * 2 + d * 4 * tc * 2 + m * tc * 14) + m * 4 * tc * 4 * 2
    return pl.pallas_call(
        functools.partial(_conv_sample_body, dec_seq=dec_seq, tc=tc),
        grid=(dc // tc,),
        in_specs=[pl.BlockSpec((m, d), lambda j: (0, 0)),
                  pl.BlockSpec((d, 4 * tc), lambda j: (0, j)),
                  pl.BlockSpec((CONV_W, tc), lambda j: (0, j)),
                  pl.BlockSpec((m, tc), lambda j: (0, j)),
                  pl.BlockSpec((m, tc), lambda j: (0, j))],
        out_specs=[pl.BlockSpec((m, tc), lambda j: (0, j)), pl.BlockSpec((m, tc), lambda j: (0, j))],
        out_shape=[jax.ShapeDtypeStruct((m, dc), BF16), jax.ShapeDtypeStruct((m, dc), F32)],
        compiler_params=_params(vmem, 1),
        name="conv_sample",
    )(xn, w4, wconv, h1, h2)


def _out_body(ma_ref, mc_ref, wa_ref, wc_ref, x_ref, g_ref, o_ref, *, tn, n_tiles):
    j = pl.program_id(1)
    z = (jnp.dot(ma_ref[...], wa_ref[...], preferred_element_type=F32)
         + jnp.dot(mc_ref[...], wc_ref[...], preferred_element_type=F32))
    o_ref[:, pl.ds(pl.multiple_of(j * tn, tn), tn)] = z

    @pl.when(j == n_tiles - 1)
    def _():
        zz = o_ref[...]
        ms = jnp.mean(zz * zz, axis=-1, keepdims=True)
        o_ref[...] = x_ref[...] + zz * lax.rsqrt(ms + EPS) * g_ref[...]


def _out_call(ma, mc, wo, x2d, g_row, tm, tn):
    m, da = ma.shape
    dcv = mc.shape[1]
    d = x2d.shape[1]
    assert da == dcv, "w_out row blocks are indexed in units of the attention width"
    n_tiles = d // tn
    vmem = 2 * (2 * tm * d * 4 + tm * (da + dcv) * 2 + (da + dcv) * tn * 2) + tm * tn * 4
    return pl.pallas_call(
        functools.partial(_out_body, tn=tn, n_tiles=n_tiles),
        grid=(m // tm, n_tiles),
        in_specs=[pl.BlockSpec((tm, da), lambda i, j: (i, 0)),
                  pl.BlockSpec((tm, dcv), lambda i, j: (i, 0)),
                  pl.BlockSpec((da, tn), lambda i, j: (0, j)),
                  pl.BlockSpec((dcv, tn), lambda i, j: (1, j)),
                  pl.BlockSpec((tm, d), lambda i, j: (i, 0)),
                  pl.BlockSpec((1, d), lambda i, j: (0, 0))],
        out_specs=pl.BlockSpec((tm, d), lambda i, j: (i, 0)),
        out_shape=jax.ShapeDtypeStruct((m, d), F32),
        compiler_params=_params(vmem, 2),
        name="out_proj",
    )(ma, mc, wo, wo, x2d, g_row)


def _attn_prompt_body(q_ref, k_ref, v_ref, c_ref, sg_ref, o_ref, vt_ref, crep_ref, *, blk, qsub):
    qi = pl.program_id(2)
    seq = k_ref.shape[1]

    @pl.when(qi == 0)
    def _():
        for c0 in range(0, seq, blk):
            vt_ref[:, c0:c0 + blk] = v_ref[0, c0:c0 + blk, :].astype(F32).T.astype(BF16)
            row = c_ref[0, 0, :, c0:c0 + blk] * LOG2E
            crep_ref[c0:c0 + blk, :] = jnp.broadcast_to(row, (LANES, blk)).T

    n_sub = blk // qsub
    q_subs = [q_ref[0, j * qsub:(j + 1) * qsub, :].astype(F32).T.astype(BF16) for j in range(n_sub)]
    c_q = c_ref[0, 0, :, pl.ds(pl.multiple_of(qi * blk, blk), blk)] * LOG2E
    cq_subs = [c_q[:, j * qsub:(j + 1) * qsub] for j in range(n_sub)]

    def tile(first_blk, width, carry, diagonal):
        ks = pl.multiple_of(first_blk * blk, blk)
        k = k_ref[0, pl.ds(ks, width), :]
        vt = vt_ref[:, pl.ds(ks, width)]
        c_k = jnp.tile(crep_ref[pl.ds(ks, width), :], (1, qsub // LANES))
        out = []
        for j, (m, l, acc) in enumerate(carry):
            st = jnp.dot(k, q_subs[j], preferred_element_type=F32) - c_k
            if diagonal:
                key = lax.broadcasted_iota(jnp.int32, (blk, qsub), 0)
                qry = lax.broadcasted_iota(jnp.int32, (blk, qsub), 1) + j * qsub
                own = jnp.where(key <= qry, st[width - blk:], NEG_INF)
                st = own if width == blk else jnp.concatenate([st[:width - blk], own], axis=0)
            m_new = jnp.maximum(m, jnp.max(st, axis=0, keepdims=True) + cq_subs[j])
            alpha = jnp.exp2(m - m_new)
            pt = jnp.exp2(st - (m_new - cq_subs[j]))
            l = alpha * l + jnp.sum(pt, axis=0, keepdims=True)
            acc = alpha * acc + jnp.dot(vt, pt.astype(BF16), preferred_element_type=F32)
            out.append((m_new, l, acc))
        return tuple(out)

    init = tuple((jnp.full((1, qsub), NEG_INF, F32), jnp.zeros((1, qsub), F32), jnp.zeros((HEAD_DIM, qsub), F32))
                 for _ in range(n_sub))
    widths = sorted(WIDE_KEY_TILES, reverse=True)
    last = sum(jnp.where(qi + 1 >= w, w - v, 0) for w, v in zip(sorted(widths), [0] + sorted(widths)[:-1]))
    n_plain = qi + 1 - last
    carry, done = init, 0
    for tiles in widths:
        n = (n_plain - done) // tiles
        carry = lax.fori_loop(0, n, functools.partial(
            lambda p, c, t, d: tile(d + p * t, t * blk, c, False), t=tiles, d=done), carry)
        done = done + n * tiles

    def last_chunk(tiles):
        return lambda c: tile(n_plain, tiles * blk, c, True)

    final = last_chunk(widths[-1])
    for tiles in sorted(widths)[1:]:
        final = functools.partial(lambda c, t, f: lax.cond(last == t, last_chunk(t), f, c), t=tiles, f=final)
    for j, (_, l, acc) in enumerate(final(carry)):
        o_ref[0, j * qsub:(j + 1) * qsub, :] = (
            (acc * (1.0 / l)).T * sg_ref[0, j * qsub:(j + 1) * qsub, :].astype(F32)).astype(o_ref.dtype)


def _attn_prompt_call(q, k, v, c_row, sg, blk):
    b, s, da = q.shape
    h = da // HEAD_DIM
    vmem = (2 * (2 * s * HEAD_DIM * 2 + 3 * blk * HEAD_DIM * 2 + s * 4 * SUBLANES)
            + s * HEAD_DIM * 2 + s * LANES * 4 + 6 * blk * blk * 4)
    return pl.pallas_call(
        functools.partial(_attn_prompt_body, blk=blk, qsub=blk),
        grid=(b, h, s // blk),
        in_specs=[pl.BlockSpec((1, blk, HEAD_DIM), lambda bi, hi, qi: (bi, qi, hi)),
                  pl.BlockSpec((1, s, HEAD_DIM), lambda bi, hi, qi: (bi, 0, hi)),
                  pl.BlockSpec((1, s, HEAD_DIM), lambda bi, hi, qi: (bi, 0, hi)),
                  pl.BlockSpec((1, 1, 1, s), lambda bi, hi, qi: (bi, hi, 0, 0)),
                  pl.BlockSpec((1, blk, HEAD_DIM), lambda bi, hi, qi: (bi, qi, hi))],
        out_specs=pl.BlockSpec((1, blk, HEAD_DIM), lambda bi, hi, qi: (bi, qi, hi)),
        out_shape=jax.ShapeDtypeStruct((b, s, da), BF16),
        scratch_shapes=[pltpu.VMEM((HEAD_DIM, s), BF16), pltpu.VMEM((s, LANES), F32)],
        compiler_params=_params(vmem, 3),
        name="attn_prompt",
    )(q, k, v, c_row, sg)


def _cum_body(pt_ref, *refs, n_pages, heads):
    del pt_ref
    x_refs, c_ref = refs[:n_pages + 1], refs[n_pages + 1]
    rows = x_refs[0].shape[-2]
    n_valid = (n_pages + 1) * rows
    n = -(-n_valid // LANES) * LANES
    blocks = [r[...] for r in x_refs]
    if n > n_valid:
        blocks.append(jnp.zeros((n - n_valid, LANES), F32))
    x = jnp.concatenate(blocks, axis=0)
    l1 = lax.broadcasted_iota(jnp.int32, (LANES, LANES), 0)
    l2 = lax.broadcasted_iota(jnp.int32, (LANES, LANES), 1)
    same_head = (l1 % heads) == (l2 % heads)
    within = same_head & ((l1 // heads) <= (l2 // heads))
    r1 = lax.broadcasted_iota(jnp.int32, (n, n), 0)
    r2 = lax.broadcasted_iota(jnp.int32, (n, n), 1)
    c = _dot_f32_by_01(x, within) + _dot_01_by_f32(r2 < r1, _dot_f32_by_01(x, same_head))
    c_ref[0] = c[:n_valid].reshape(n_pages + 1, rows, LANES)


def _cum_call(page_table, logf_pages, layer, new_rows, heads):
    db, n_pages = page_table.shape
    rows = logf_pages.shape[2]

    def page_map(p):
        return lambda s, pt: (layer, pt[s, p], 0, 0)

    in_specs = [pl.BlockSpec((None, None, rows, LANES), page_map(p)) for p in range(n_pages)]
    in_specs.append(pl.BlockSpec((None, rows, LANES), lambda s, pt: (s, 0, 0)))
    n = (n_pages + 1) * rows
    return pl.pallas_call(
        functools.partial(_cum_body, n_pages=n_pages, heads=heads),
        grid_spec=pltpu.PrefetchScalarGridSpec(
            num_scalar_prefetch=1, grid=(db,), in_specs=in_specs,
            out_specs=pl.BlockSpec((1, n_pages + 1, rows, LANES), lambda s, pt: (s, 0, 0, 0))),
        out_shape=jax.ShapeDtypeStruct((db, n_pages + 1, rows, LANES), F32),
        compiler_params=_params(4 * n * LANES * 4 + n * n * 8, 1),
        name="cache_logf_cumsum",
    )(page_table, *([logf_pages] * n_pages), new_rows)


def _attn_sample_body(pt_ref, kc_ref, vc_ref, q_ref, c_ref, cn_ref, cq_ref, kn_ref, vn_ref, sg_ref, o_ref,
                      kbuf, vbuf, sem, m_ref, l_ref, acc_ref, *, layer, group, heads, dec_seq, n_groups, n_slots):
    g = pl.program_id(1)
    step = pl.program_id(0) * n_groups + g
    n_steps = pl.num_programs(0) * n_groups
    rows = heads * dec_seq
    q = q_ref[0]
    cq = cq_ref[0] * LOG2E

    def page_copies(t, slot):
        seq = t // n_groups
        first = (t - seq * n_groups) * group
        copies = []
        for i in range(group):
            page = pt_ref[seq, first + i]
            copies.append(pltpu.make_async_copy(kc_ref.at[layer, page], kbuf.at[slot, i], sem.at[0, slot]))
            copies.append(pltpu.make_async_copy(vc_ref.at[layer, page], vbuf.at[slot, i], sem.at[1, slot]))
        return copies

    @pl.when(step == 0)
    def _():
        for t in range(n_slots - 1):
            for cp in page_copies(t, t):
                cp.start()

    ahead = step + (n_slots - 1)

    @pl.when(ahead < n_steps)
    def _():
        for cp in page_copies(ahead, ahead % n_slots):
            cp.start()

    slot = step % n_slots
    for cp in page_copies(step, slot):
        cp.wait()

    @pl.when(g == 0)
    def _():
        m_ref[...] = jnp.full(m_ref.shape, NEG_INF, F32)
        l_ref[...] = jnp.zeros_like(l_ref)
        acc_ref[...] = jnp.zeros_like(acc_ref)

    r_head = lax.broadcasted_iota(jnp.int32, (rows, LANES), 0) % heads
    lane = lax.broadcasted_iota(jnp.int32, (rows, LANES), 1)
    same_head = (lane % heads) == r_head

    def local(kf, vf, bias_tiles):
        s = lax.dot_general(q, kf, (((1,), (1,)), ((), ())), preferred_element_type=F32)
        s = jnp.concatenate([s[:, t * LANES:(t + 1) * LANES] + bias_tiles[t] for t in range(len(bias_tiles))], axis=1)
        m_blk = jnp.max(s, axis=-1, keepdims=True)
        p = jnp.exp2(s - m_blk)
        return m_blk, jnp.sum(p, axis=-1, keepdims=True), jnp.dot(p.astype(BF16), vf, preferred_element_type=F32)

    def merge(parts):
        m, l, acc = m_ref[...], l_ref[...], acc_ref[...]
        for m_blk, l_blk, o_blk in parts:
            m_new = jnp.maximum(m, m_blk + cq)
            a_old, a_blk = jnp.exp2(m - m_new), jnp.exp2(m_blk + cq - m_new)
            l = a_old * l + a_blk * l_blk
            acc = a_old * acc + a_blk * o_blk
            m = m_new
        m_ref[...], l_ref[...], acc_ref[...] = m, l, acc

    head_mask = jnp.where(same_head, 0.0, NEG_INF).astype(F32)
    tiles_per_page = kbuf.shape[2] * heads // LANES
    parts = []
    for i in range(group):
        kf = kbuf[slot, i].reshape(-1, HEAD_DIM).astype(BF16)
        vf = vbuf[slot, i].reshape(-1, HEAD_DIM).astype(BF16)
        c = c_ref[0, i] * LOG2E
        parts.append(local(kf, vf, [head_mask - c[t:t + 1, :] for t in range(tiles_per_page)]))
    merge(parts)

    @pl.when(g == n_groups - 1)
    def _():
        r_query = lax.broadcasted_iota(jnp.int32, (rows, LANES), 0) // heads
        ok = same_head & ((lane // heads) <= r_query) & (lane < rows)
        bias = jnp.where(ok, 0.0, NEG_INF).astype(F32) - cn_ref[0, 0][0:1, :] * LOG2E
        merge([local(kn_ref[0], vn_ref[0], [bias])])
        o_ref[0] = acc_ref[...] * (1.0 / l_ref[...]) * sg_ref[0].astype(F32)


def _attn_sample_call(page_table, cache_k, cache_v, layer, q_rows, c_all, cq_col, kn_rows, vn_rows, sg_rows,
                      group):
    db, n_pages = page_table.shape
    _, _, page, heads, hd = cache_k.shape
    rows = q_rows.shape[1]
    dec_seq = rows // heads
    n_groups = n_pages // group
    tiles_per_page = page * heads // LANES

    n_slots = RING_SLOTS
    assert db * n_groups >= n_slots - 1
    in_specs = [
        pl.BlockSpec(memory_space=pl.ANY),
        pl.BlockSpec(memory_space=pl.ANY),
        pl.BlockSpec((1, rows, hd), lambda s, g, pt: (s, 0, 0)),
        pl.BlockSpec((1, group, tiles_per_page, LANES), lambda s, g, pt: (s, g, 0, 0)),
        pl.BlockSpec((1, 1, tiles_per_page, LANES), lambda s, g, pt: (s, n_pages, 0, 0)),
        pl.BlockSpec((1, rows, 1), lambda s, g, pt: (s, 0, 0)),
        pl.BlockSpec((1, LANES, hd), lambda s, g, pt: (s, 0, 0)),
        pl.BlockSpec((1, LANES, hd), lambda s, g, pt: (s, 0, 0)),
        pl.BlockSpec((1, rows, hd), lambda s, g, pt: (s, 0, 0)),
    ]
    ring = (n_slots, group, page, heads, hd)
    vmem = 2 * n_slots * group * page * heads * hd * 4 + 4 * rows * page * heads * 4 + (4 << 20)
    return pl.pallas_call(
        functools.partial(_attn_sample_body, layer=layer, group=group, heads=heads, dec_seq=dec_seq,
                          n_groups=n_groups, n_slots=n_slots),
        grid_spec=pltpu.PrefetchScalarGridSpec(
            num_scalar_prefetch=1, grid=(db, n_groups), in_specs=in_specs,
            out_specs=pl.BlockSpec((1, rows, hd), lambda s, g, pt: (s, 0, 0)),
            scratch_shapes=[pltpu.VMEM(ring, F32), pltpu.VMEM(ring, F32), pltpu.SemaphoreType.DMA((2, n_slots)),
                            pltpu.VMEM((rows, 1), F32), pltpu.VMEM((rows, 1), F32), pltpu.VMEM((rows, hd), F32)]),
        out_shape=jax.ShapeDtypeStruct((db, rows, hd), F32),
        compiler_params=_params(vmem, 2),
        name="attn_sample",
    )(page_table, cache_k, cache_v, q_rows, c_all, c_all, cq_col, kn_rows, vn_rows, sg_rows)


def _pick(n, pref):
    t = min(pref, n)
    assert n % t == 0, (n, t)
    return t


def _layer(xp, xs, cache_k, cache_v, logf_pages, state_conv_l, page_table, layer,
           g_pre, w_t, b_f, w_conv, w_out, g_post):
    b, s, d = xp.shape
    db, ds, _ = xs.shape
    heads = b_f.shape[0]
    da = heads * HEAD_DIM
    dc = w_conv.shape[1]
    page = cache_k.shape[2]
    n_pages = page_table.shape[1]
    mp, msz = b * s, db * ds
    scale = HEAD_DIM ** -0.5 * LOG2E

    tc = _pick(dc, 256)
    n_ct = dc // tc
    off = 3 * da + heads
    assert w_t.shape[1] == off + da + 4 * dc and _pick(4 * dc, 256) == tc
    wqkv = _weights_call(w_t, layer, 0, 3 * da, lambda i: i, "weights_qkv")
    wga = _weights_call(w_t, layer, off, da, lambda i: i, "weights_gate")
    w4 = _weights_call(w_t, layer, off + da, 4 * dc, lambda i: 4 * (i % n_ct) + i // n_ct, "weights_conv")
    wf = jnp.pad(w_t[layer, 3 * da:off], ((0, LANES - heads), (0, 0))).astype(BF16)
    bf = jnp.pad(b_f, (0, LANES - heads)).reshape(1, LANES).astype(F32)
    wo = w_out.astype(BF16)
    gpre = g_pre.reshape(1, d)
    gpost = g_post.reshape(1, d)

    def project(x2d, tm):
        xn = _rmsnorm_bf16(x2d, gpre)
        tn = _pick(da, 1024)
        (qv,) = _proj_call(functools.partial(_q_body, scale=scale), xn, wqkv, [BF16], tm, tn, "proj_q", da, 0)
        k32, k16 = _proj_call(_kv_body, xn, wqkv, [F32, BF16], tm, tn, "proj_k", da, da)
        v32, v16 = _proj_call(_kv_body, xn, wqkv, [F32, BF16], tm, tn, "proj_v", da, 2 * da)
        (sg,) = _proj_call(_gate_body, xn, wga, [BF16], tm, tn, "proj_gate")
        return xn, qv, k32, k16, v32, v16, sg

    xp2 = xp.reshape(mp, d)
    tm_p = _pick(s, 512)
    xn, qv, k32, k16, v32, v16, sg = project(xp2, tm_p)
    tm_f = _pick(s, 256)
    logf_pad, c_pad = _forget_call(xn, wf, bf, s // tm_f, tm_f)
    logf_p = logf_pad[:, :heads].reshape(b, s, heads)
    c_p = c_pad[:, :heads].reshape(b, s, heads)
    c_row = c_p.transpose(0, 2, 1).reshape(b, heads, 1, s)
    blk = _pick(s, 512)
    ma_p = _attn_prompt_call(qv.reshape(b, s, da), k16.reshape(b, s, da), v16.reshape(b, s, da),
                             c_row, sg.reshape(b, s, da), blk)
    mc_p, conv_p = _conv_prompt_call(xn, w4, w_conv, b, tm_p, tc)
    tm_o = _pick(s, 512)
    yp = _out_call(ma_p.reshape(mp, da), mc_p, wo, xp2, gpost, tm_o, _pick(d, 512)).reshape(b, s, d)
    k_p = k32.reshape(b, s, heads, HEAD_DIM)
    v_p = v32.reshape(b, s, heads, HEAD_DIM)

    xs2 = xs.reshape(msz, d)
    tm_s = _pick(msz, 512)
    xn, qv, k32, k16, v32, v16, sg = project(xs2, tm_s)
    logf_pad, _ = _forget_call(xn, wf, bf, 1, tm_s)
    logf_s = logf_pad[:, :heads].reshape(db, ds, heads)
    rows = ds * heads
    tiles_per_page = page * heads // LANES
    new_rows = jnp.pad(logf_s.reshape(db, 1, rows), ((0, 0), (0, tiles_per_page - 1), (0, LANES - rows)))
    c_all = _cum_call(page_table, logf_pages, layer, new_rows, heads)
    cq_s = c_all[:, n_pages, 0, :rows].reshape(db, rows, 1)
    pad_rows = ((0, 0), (0, LANES - rows), (0, 0))
    kn = jnp.pad(k16.reshape(db, rows, HEAD_DIM), pad_rows)
    vn = jnp.pad(v16.reshape(db, rows, HEAD_DIM), pad_rows)
    o_s = _attn_sample_call(page_table, cache_k, cache_v, layer, qv.reshape(db, rows, HEAD_DIM), c_all, cq_s,
                            kn, vn, sg.reshape(db, rows, HEAD_DIM), _pick(n_pages, 4))
    ma_s = o_s.reshape(msz, da).astype(BF16)
    hist = state_conv_l
    zeros = jnp.zeros((db, 1, dc), F32)
    h1 = jnp.concatenate([hist[:, 1:2]] + [zeros] * (ds - 1), axis=1).reshape(msz, dc)
    h2 = jnp.concatenate([hist[:, 0:1], hist[:, 1:2]] + [zeros] * (ds - 2), axis=1).reshape(msz, dc)
    mc_s, u_s = _conv_sample_call(xn, w4, w_conv, h1, h2, ds, tc)
    ys = _out_call(ma_s, mc_s, wo, xs2, gpost, _pick(msz, 512), _pick(d, 512)).reshape(db, ds, d)
    conv_s = u_s.reshape(db, ds, dc)[:, ds - (CONV_W - 1):]
    k_s = k32.reshape(db, ds, heads, HEAD_DIM)
    v_s = v32.reshape(db, ds, heads, HEAD_DIM)
    return yp, ys, (k_p, v_p, logf_p, conv_p, k_s, v_s, logf_s, conv_s)


def kernel(x_prompt, x_sample, cache_k, cache_v, cache_logf, state_conv, page_table, g_pre, w_in, b_f, w_conv,
           w_out, g_post):
    depth, n_pool, page, heads = cache_logf.shape
    assert cache_k.shape[-1] == HEAD_DIM and (page * heads) % LANES == 0 and LANES % heads == 0
    assert x_sample.shape[1] >= CONV_W - 1 and x_sample.shape[1] * heads <= LANES
    logf_pages = cache_logf.reshape(depth, n_pool, page * heads // LANES, LANES)
    w_t = jnp.swapaxes(w_in, 1, 2)
    xp, xs = x_prompt, x_sample
    per_layer = []
    for layer in range(depth):
        xp, xs, outs = _layer(xp, xs, cache_k, cache_v, logf_pages, state_conv[layer], page_table, layer,
                              g_pre[layer], w_t, b_f[layer], w_conv[layer], w_out[layer], g_post[layer])
        per_layer.append(outs)
    stacked = [jnp.stack(leaf) for leaf in zip(*per_layer)]
    return (xp, xs, *stacked)
```

```python
import functools

import jax
import jax.numpy as jnp
from jax import lax
from jax.experimental import pallas as pl
from jax.experimental.pallas import tpu as pltpu

HEAD_DIM = 128
CONV_W = 3
EPS = 1e-6
LANES = 128
SUBLANES = 8
V7X_VMEM_BYTES = 64 * 1024 * 1024
VMEM_BUDGET_BYTES = 58 * 1024 * 1024
F32 = jnp.float32
BF16 = jnp.bfloat16
NEG_INF = float("-inf")
LOG2E = 1.4426950408889634
WIDE_KEY_TILES = (4, 2, 1)
RING_SLOTS = 4


def _params(vmem_bytes, n_grid):
    limit = min(int(vmem_bytes * 1.25) + (4 << 20), VMEM_BUDGET_BYTES)
    return pltpu.CompilerParams(dimension_semantics=("arbitrary",) * n_grid, vmem_limit_bytes=limit)


def _silu(z):
    return z * (1.0 / (1.0 + jnp.exp(-z)))


def _split3(x):
    hi = x.astype(BF16)
    r = x - hi.astype(F32)
    mid = r.astype(BF16)
    lo = (r - mid.astype(F32)).astype(BF16)
    return hi, mid, lo


def _dot_f32_by_01(x, w01):
    w = w01.astype(BF16)
    hi, mid, lo = (jnp.dot(p, w, preferred_element_type=F32) for p in _split3(x))
    return (lo + mid) + hi


def _dot_01_by_f32(w01, x):
    w = w01.astype(BF16)
    hi, mid, lo = (jnp.dot(w, p, preferred_element_type=F32) for p in _split3(x))
    return (lo + mid) + hi


def _rmsnorm_kernel(x_ref, g_ref, o_ref):
    x = x_ref[...]
    ms = jnp.mean(x * x, axis=-1, keepdims=True)
    o_ref[...] = (x * lax.rsqrt(ms + EPS) * g_ref[...]).astype(o_ref.dtype)


def _rmsnorm_bf16(x2d, g_row):
    m, d = x2d.shape
    tm = min(256, m)
    return pl.pallas_call(
        _rmsnorm_kernel,
        grid=(m // tm,),
        in_specs=[pl.BlockSpec((tm, d), lambda i: (i, 0)), pl.BlockSpec((1, d), lambda i: (0, 0))],
        out_specs=pl.BlockSpec((tm, d), lambda i: (i, 0)),
        out_shape=jax.ShapeDtypeStruct((m, d), BF16),
        compiler_params=_params(2 * tm * d * 6, 1),
        name="rmsnorm_pre",
    )(x2d, g_row)


def _weights_body(w_ref, o_ref):
    o_ref[...] = w_ref[0].T.astype(BF16)


def _weights_call(w_t, layer, row0, n_rows, out_block_of, name):
    d = w_t.shape[2]
    tr = _pick(n_rows, 256)
    assert row0 % SUBLANES == 0
    return pl.pallas_call(
        _weights_body,
        grid=(n_rows // tr,),
        in_specs=[pl.BlockSpec((pl.Element(1), pl.Element(tr), pl.Element(d)), lambda i: (layer, pl.multiple_of(row0 + i * tr, SUBLANES), 0))],
        out_specs=pl.BlockSpec((d, tr), lambda i: (0, out_block_of(i))),
        out_shape=jax.ShapeDtypeStruct((d, n_rows), BF16),
        compiler_params=_params(2 * tr * d * 6 + tr * d * 8, 1),
        name=name,
    )(w_t)


def _proj_call(body, xn, w, out_dtypes, tm, tn, name, n=None, col0=0, last_transposed=False):
    m, d = xn.shape
    n = w.shape[1] if n is None else n
    assert col0 % tn == 0 and n % tn == 0
    jb = col0 // tn
    out_bytes = sum(jnp.dtype(t).itemsize for t in out_dtypes)
    vmem = 2 * (tm * d * 2 + d * tn * 2 + tm * tn * out_bytes) + tm * tn * 4
    out_specs = [pl.BlockSpec((tm, tn), lambda j, i: (i, j)) for _ in out_dtypes]
    out_shape = [jax.ShapeDtypeStruct((m, n), t) for t in out_dtypes]
    if last_transposed:
        out_specs[-1] = pl.BlockSpec((tn, tm), lambda j, i: (j, i))
        out_shape[-1] = jax.ShapeDtypeStruct((n, m), out_dtypes[-1])
    return pl.pallas_call(
        body,
        grid=(n // tn, m // tm),
        in_specs=[pl.BlockSpec((tm, d), lambda j, i: (i, 0)), pl.BlockSpec((d, tn), lambda j, i: (0, j + jb))],
        out_specs=out_specs,
        out_shape=out_shape,
        compiler_params=_params(vmem, 2),
        name=name,
    )(xn, w)


def _q_body(xn_ref, w_ref, o_ref, *, scale):
    z = jnp.dot(xn_ref[...], w_ref[...], preferred_element_type=F32)
    o_ref[...] = (z * scale).astype(o_ref.dtype)


def _kv_body(xn_ref, w_ref, o32_ref, o16_ref):
    z = jnp.dot(xn_ref[...], w_ref[...], preferred_element_type=F32)
    o32_ref[...] = z
    o16_ref[...] = z.astype(BF16)


def _vt_body(xn_ref, w_ref, o32_ref, o16t_ref):
    z = jnp.dot(xn_ref[...], w_ref[...], preferred_element_type=F32)
    o32_ref[...] = z
    o16t_ref[...] = z.T.astype(BF16)


def _gate_body(xn_ref, w_ref, o_ref):
    z = jnp.dot(xn_ref[...], w_ref[...], preferred_element_type=F32)
    o_ref[...] = _silu(z).astype(o_ref.dtype)


def _log_sigmoid(z):
    return jnp.minimum(z, 0.0) - jnp.log1p(jnp.exp(-jnp.abs(z)))


def _forget_body(xn_ref, w_ref, b_ref, logf_ref, c_ref, carry_ref, *, tiles_per_seq):
    i = pl.program_id(0)
    tm = xn_ref.shape[0]
    z = lax.dot_general(xn_ref[...], w_ref[...], (((1,), (1,)), ((), ())), preferred_element_type=F32)
    logf = _log_sigmoid(z + b_ref[...])
    logf_ref[...] = logf

    @pl.when(i % tiles_per_seq == 0)
    def _():
        carry_ref[...] = jnp.zeros_like(carry_ref)

    row = lax.broadcasted_iota(jnp.int32, (tm, tm), 0)
    col = lax.broadcasted_iota(jnp.int32, (tm, tm), 1)
    c = _dot_01_by_f32(col <= row, logf) + carry_ref[0:1, :]
    c_ref[...] = c
    carry_ref[...] = jnp.broadcast_to(c[tm - 1:tm, :], carry_ref.shape)


def _forget_call(xn, wf_pad, bf_pad, tiles_per_seq, tm):
    m, d = xn.shape
    vmem = 2 * (tm * d * 2 + d * LANES * 2 + 2 * tm * LANES * 4) + tm * tm * 8
    return pl.pallas_call(
        functools.partial(_forget_body, tiles_per_seq=tiles_per_seq),
        grid=(m // tm,),
        in_specs=[pl.BlockSpec((tm, d), lambda i: (i, 0)),
                  pl.BlockSpec((LANES, d), lambda i: (0, 0)),
                  pl.BlockSpec((1, LANES), lambda i: (0, 0))],
        out_specs=[pl.BlockSpec((tm, LANES), lambda i: (i, 0)), pl.BlockSpec((tm, LANES), lambda i: (i, 0))],
        out_shape=[jax.ShapeDtypeStruct((m, LANES), F32), jax.ShapeDtypeStruct((m, LANES), F32)],
        scratch_shapes=[pltpu.VMEM((SUBLANES, LANES), F32)],
        compiler_params=_params(vmem, 1),
        name="forget_gate",
    )(xn, wf_pad, bf_pad)


def _conv_tail(zb, zg, u, u1, u2, wc_ref, m_ref):
    wc = wc_ref[...]
    y = wc[0:1, :] * u2 + wc[1:2, :] * u1 + wc[2:3, :] * u
    m_ref[...] = (zb * y * _silu(zg)).astype(m_ref.dtype)


def _conv_prompt_body(xn_ref, w_ref, wc_ref, m_ref, cn_ref, carry_ref, *, tiles_per_seq, tc):
    i = pl.program_id(1)
    tm = xn_ref.shape[0]
    z = jnp.dot(xn_ref[...], w_ref[...], preferred_element_type=F32)
    zb, zc, zh, zg = (z[:, k * tc:(k + 1) * tc] for k in range(4))
    u = zc * zh

    @pl.when(i % tiles_per_seq == 0)
    def _():
        carry_ref[...] = jnp.zeros_like(carry_ref)

    prev = carry_ref[...]
    p1 = prev[SUBLANES - 1:SUBLANES, :]
    p2 = prev[SUBLANES - 2:SUBLANES - 1, :]
    row = lax.broadcasted_iota(jnp.int32, (tm, tc), 0)
    u1 = jnp.where(row >= 1, pltpu.roll(u, 1, axis=0), p1)
    u2 = jnp.where(row >= 2, pltpu.roll(u, 2, axis=0), jnp.where(row == 1, p1, p2))
    carry_ref[...] = u[tm - SUBLANES:tm, :]
    cn_ref[0] = u[tm - (CONV_W - 1):tm, :]
    _conv_tail(zb, zg, u, u1, u2, wc_ref, m_ref)


def _conv_sample_body(xn_ref, w_ref, wc_ref, h1_ref, h2_ref, m_ref, u_ref, *, dec_seq, tc):
    tm = xn_ref.shape[0]
    z = jnp.dot(xn_ref[...], w_ref[...], preferred_element_type=F32)
    zb, zc, zh, zg = (z[:, k * tc:(k + 1) * tc] for k in range(4))
    u = zc * zh
    u_ref[...] = u
    t = lax.rem(lax.broadcasted_iota(jnp.int32, (tm, tc), 0), dec_seq)
    u1 = jnp.where(t >= 1, pltpu.roll(u, 1, axis=0), h1_ref[...])
    u2 = jnp.where(t >= 2, pltpu.roll(u, 2, axis=0), h2_ref[...])
    _conv_tail(zb, zg, u, u1, u2, wc_ref, m_ref)


def _conv_prompt_call(xn, w4, wconv, batch, tm, tc):
    m, d = xn.shape
    dc = wconv.shape[1]
    tiles_per_seq = m // batch // tm
    vmem = 2 * (tm * d * 2 + d * 4 * tc * 2 + tm * tc * 2) + tm * 4 * tc * 4 * 2
    return pl.pallas_call(
        functools.partial(_conv_prompt_body, tiles_per_seq=tiles_per_seq, tc=tc),
        grid=(dc // tc, m // tm),
        in_specs=[pl.BlockSpec((tm, d), lambda j, i: (i, 0)),
                  pl.BlockSpec((d, 4 * tc), lambda j, i: (0, j)),
                  pl.BlockSpec((CONV_W, tc), lambda j, i: (0, j))],
        out_specs=[pl.BlockSpec((tm, tc), lambda j, i: (i, j)),
                   pl.BlockSpec((1, CONV_W - 1, tc), lambda j, i: (i // tiles_per_seq, 0, j))],
        out_shape=[jax.ShapeDtypeStruct((m, dc), BF16), jax.ShapeDtypeStruct((batch, CONV_W - 1, dc), F32)],
        scratch_shapes=[pltpu.VMEM((SUBLANES, tc), F32)],
        compiler_params=_params(vmem, 2),
        name="conv_prompt",
    )(xn, w4, wconv)


def _conv_sample_call(xn, w4, wconv, h1, h2, dec_seq, tc):
    m, d = xn.shape
    dc = wconv.shape[1]
    vmem = 2 * (m * d * 2 + d * 4 * tc * 2 + m * tc * 14) + m * 4 * tc * 4 * 2
    return pl.pallas_call(
        functools.partial(_conv_sample_body, dec_seq=dec_seq, tc=tc),
        grid=(dc // tc,),
        in_specs=[pl.BlockSpec((m, d), lambda j: (0, 0)),
                  pl.BlockSpec((d, 4 * tc), lambda j: (0, j)),
                  pl.BlockSpec((CONV_W, tc), lambda j: (0, j)),
                  pl.BlockSpec((m, tc), lambda j: (0, j)),
                  pl.BlockSpec((m, tc), lambda j: (0, j))],
        out_specs=[pl.BlockSpec((m, tc), lambda j: (0, j)), pl.BlockSpec((m, tc), lambda j: (0, j))],
        out_shape=[jax.ShapeDtypeStruct((m, dc), BF16), jax.ShapeDtypeStruct((m, dc), F32)],
        compiler_params=_params(vmem, 1),
        name="conv_sample",
    )(xn, w4, wconv, h1, h2)


def _out_body(ma_ref, mc_ref, wa_ref, wc_ref, x_ref, g_ref, o_ref, *, tn, n_tiles):
    j = pl.program_id(1)
    z = (jnp.dot(ma_ref[...], wa_ref[...], preferred_element_type=F32)
         + jnp.dot(mc_ref[...], wc_ref[...], preferred_element_type=F32))
    o_ref[:, pl.ds(pl.multiple_of(j * tn, tn), tn)] = z

    @pl.when(j == n_tiles - 1)
    def _():
        zz = o_ref[...]
        ms = jnp.mean(zz * zz, axis=-1, keepdims=True)
        o_ref[...] = x_ref[...] + zz * lax.rsqrt(ms + EPS) * g_ref[...]


def _out_call(ma, mc, wo, x2d, g_row, tm, tn):
    m, da = ma.shape
    dcv = mc.shape[1]
    d = x2d.shape[1]
    assert da == dcv, "w_out row blocks are indexed in units of the attention width"
    n_tiles = d // tn
    vmem = 2 * (2 * tm * d * 4 + tm * (da + dcv) * 2 + (da + dcv) * tn * 2) + tm * tn * 4
    return pl.pallas_call(
        functools.partial(_out_body, tn=tn, n_tiles=n_tiles),
        grid=(m // tm, n_tiles),
        in_specs=[pl.BlockSpec((tm, da), lambda i, j: (i, 0)),
                  pl.BlockSpec((tm, dcv), lambda i, j: (i, 0)),
                  pl.BlockSpec((da, tn), lambda i, j: (0, j)),
                  pl.BlockSpec((dcv, tn), lambda i, j: (1, j)),
                  pl.BlockSpec((tm, d), lambda i, j: (i, 0)),
                  pl.BlockSpec((1, d), lambda i, j: (0, 0))],
        out_specs=pl.BlockSpec((tm, d), lambda i, j: (i, 0)),
        out_shape=jax.ShapeDtypeStruct((m, d), F32),
        compiler_params=_params(vmem, 2),
        name="out_proj",
    )(ma, mc, wo, wo, x2d, g_row)


def _attn_prompt_body(q_ref, k_ref, vt_ref, c_ref, sg_ref, o_ref, crep_ref, *, blk, qsub):
    qi = pl.program_id(2)
    seq = k_ref.shape[1]

    @pl.when(qi == 0)
    def _():
        for c0 in range(0, seq, blk):
            row = c_ref[0, 0, :, c0:c0 + blk] * LOG2E
            crep_ref[c0:c0 + blk, :] = jnp.broadcast_to(row, (LANES, blk)).T

    n_sub = blk // qsub
    q_subs = [q_ref[0, j * qsub:(j + 1) * qsub, :].astype(F32).T.astype(BF16) for j in range(n_sub)]
    c_q = c_ref[0, 0, :, pl.ds(pl.multiple_of(qi * blk, blk), blk)] * LOG2E
    cq_subs = [c_q[:, j * qsub:(j + 1) * qsub] for j in range(n_sub)]

    def tile(first_blk, width, carry, diagonal):
        ks = pl.multiple_of(first_blk * blk, blk)
        k = k_ref[0, pl.ds(ks, width), :]
        vt = vt_ref[:, pl.ds(ks, width)]
        c_k = jnp.tile(crep_ref[pl.ds(ks, width), :], (1, qsub // LANES))
        out = []
        for j, (m, l, acc) in enumerate(carry):
            st = jnp.dot(k, q_subs[j], preferred_element_type=F32) - c_k
            if diagonal:
                key = lax.broadcasted_iota(jnp.int32, (blk, qsub), 0)
                qry = lax.broadcasted_iota(jnp.int32, (blk, qsub), 1) + j * qsub
                own = jnp.where(key <= qry, st[width - blk:], NEG_INF)
                st = own if width == blk else jnp.concatenate([st[:width - blk], own], axis=0)
            m_new = jnp.maximum(m, jnp.max(st, axis=0, keepdims=True) + cq_subs[j])
            alpha = jnp.exp2(m - m_new)
            pt = jnp.exp2(st - (m_new - cq_subs[j]))
            l = alpha * l + jnp.sum(pt, axis=0, keepdims=True)
            acc = alpha * acc + jnp.dot(vt, pt.astype(BF16), preferred_element_type=F32)
            out.append((m_new, l, acc))
        return tuple(out)

    init = tuple((jnp.full((1, qsub), NEG_INF, F32), jnp.zeros((1, qsub), F32), jnp.zeros((HEAD_DIM, qsub), F32))
                 for _ in range(n_sub))
    widths = sorted(WIDE_KEY_TILES, reverse=True)
    last = sum(jnp.where(qi + 1 >= w, w - v, 0) for w, v in zip(sorted(widths), [0] + sorted(widths)[:-1]))
    n_plain = qi + 1 - last
    carry, done = init, 0
    for tiles in widths:
        n = (n_plain - done) // tiles
        carry = lax.fori_loop(0, n, functools.partial(
            lambda p, c, t, d: tile(d + p * t, t * blk, c, False), t=tiles, d=done), carry)
        done = done + n * tiles

    def last_chunk(tiles):
        return lambda c: tile(n_plain, tiles * blk, c, True)

    final = last_chunk(widths[-1])
    for tiles in sorted(widths)[1:]:
        final = functools.partial(lambda c, t, f: lax.cond(last == t, last_chunk(t), f, c), t=tiles, f=final)
    for j, (_, l, acc) in enumerate(final(carry)):
        o_ref[0, j * qsub:(j + 1) * qsub, :] = (
            (acc * (1.0 / l)).T * sg_ref[0, j * qsub:(j + 1) * qsub, :].astype(F32)).astype(o_ref.dtype)


def _attn_prompt_call(q, k, v_t, c_row, sg, blk):
    b, s, da = q.shape
    h = da // HEAD_DIM
    vmem = (2 * (2 * s * HEAD_DIM * 2 + 3 * blk * HEAD_DIM * 2 + s * 4 * SUBLANES)
            + s * HEAD_DIM * 2 + s * LANES * 4 + 6 * blk * blk * 4)
    return pl.pallas_call(
        functools.partial(_attn_prompt_body, blk=blk, qsub=blk),
        grid=(b, h, s // blk),
        in_specs=[pl.BlockSpec((1, blk, HEAD_DIM), lambda bi, hi, qi: (bi, qi, hi)),
                  pl.BlockSpec((1, s, HEAD_DIM), lambda bi, hi, qi: (bi, 0, hi)),
                  pl.BlockSpec((HEAD_DIM, s), lambda bi, hi, qi: (hi, bi)),
                  pl.BlockSpec((1, 1, 1, s), lambda bi, hi, qi: (bi, hi, 0, 0)),
                  pl.BlockSpec((1, blk, HEAD_DIM), lambda bi, hi, qi: (bi, qi, hi))],
        out_specs=pl.BlockSpec((1, blk, HEAD_DIM), lambda bi, hi, qi: (bi, qi, hi)),
        out_shape=jax.ShapeDtypeStruct((b, s, da), BF16),
        scratch_shapes=[pltpu.VMEM((s, LANES), F32)],
        compiler_params=_params(vmem, 3),
        name="attn_prompt",
    )(q, k, v_t, c_row, sg)


def _cum_body(pt_ref, *refs, n_pages, heads):
    del pt_ref
    x_refs, c_ref = refs[:n_pages + 1], refs[n_pages + 1]
    rows = x_refs[0].shape[-2]
    n_valid = (n_pages + 1) * rows
    n = -(-n_valid // LANES) * LANES
    blocks = [r[...] for r in x_refs]
    if n > n_valid:
        blocks.append(jnp.zeros((n - n_valid, LANES), F32))
    x = jnp.concatenate(blocks, axis=0)
    l1 = lax.broadcasted_iota(jnp.int32, (LANES, LANES), 0)
    l2 = lax.broadcasted_iota(jnp.int32, (LANES, LANES), 1)
    same_head = (l1 % heads) == (l2 % heads)
    within = same_head & ((l1 // heads) <= (l2 // heads))
    r1 = lax.broadcasted_iota(jnp.int32, (n, n), 0)
    r2 = lax.broadcasted_iota(jnp.int32, (n, n), 1)
    c = _dot_f32_by_01(x, within) + _dot_01_by_f32(r2 < r1, _dot_f32_by_01(x, same_head))
    c_ref[0] = c[:n_valid].reshape(n_pages + 1, rows, LANES)


def _cum_call(page_table, logf_pages, layer, new_rows, heads):
    db, n_pages = page_table.shape
    rows = logf_pages.shape[2]

    def page_map(p):
        return lambda s, pt: (layer, pt[s, p], 0, 0)

    in_specs = [pl.BlockSpec((None, None, rows, LANES), page_map(p)) for p in range(n_pages)]
    in_specs.append(pl.BlockSpec((None, rows, LANES), lambda s, pt: (s, 0, 0)))
    n = (n_pages + 1) * rows
    return pl.pallas_call(
        functools.partial(_cum_body, n_pages=n_pages, heads=heads),
        grid_spec=pltpu.PrefetchScalarGridSpec(
            num_scalar_prefetch=1, grid=(db,), in_specs=in_specs,
            out_specs=pl.BlockSpec((1, n_pages + 1, rows, LANES), lambda s, pt: (s, 0, 0, 0))),
        out_shape=jax.ShapeDtypeStruct((db, n_pages + 1, rows, LANES), F32),
        compiler_params=_params(4 * n * LANES * 4 + n * n * 8, 1),
        name="cache_logf_cumsum",
    )(page_table, *([logf_pages] * n_pages), new_rows)


def _attn_sample_body(pt_ref, kc_ref, vc_ref, q_ref, c_ref, cn_ref, cq_ref, kn_ref, vn_ref, sg_ref, o_ref,
                      kbuf, vbuf, sem, m_ref, l_ref, acc_ref, *, layer, group, heads, dec_seq, n_groups, n_slots):
    g = pl.program_id(1)
    step = pl.program_id(0) * n_groups + g
    n_steps = pl.num_programs(0) * n_groups
    rows = heads * dec_seq
    q = q_ref[0]
    cq = cq_ref[0] * LOG2E

    def page_copies(t, slot):
        seq = t // n_groups
        first = (t - seq * n_groups) * group
        copies = []
        for i in range(group):
            page = pt_ref[seq, first + i]
            copies.append(pltpu.make_async_copy(kc_ref.at[layer, page], kbuf.at[slot, i], sem.at[0, slot]))
            copies.append(pltpu.make_async_copy(vc_ref.at[layer, page], vbuf.at[slot, i], sem.at[1, slot]))
        return copies

    @pl.when(step == 0)
    def _():
        for t in range(n_slots - 1):
            for cp in page_copies(t, t):
                cp.start()

    ahead = step + (n_slots - 1)

    @pl.when(ahead < n_steps)
    def _():
        for cp in page_copies(ahead, ahead % n_slots):
            cp.start()

    slot = step % n_slots
    for cp in page_copies(step, slot):
        cp.wait()

    @pl.when(g == 0)
    def _():
        m_ref[...] = jnp.full(m_ref.shape, NEG_INF, F32)
        l_ref[...] = jnp.zeros_like(l_ref)
        acc_ref[...] = jnp.zeros_like(acc_ref)

    r_head = lax.broadcasted_iota(jnp.int32, (rows, LANES), 0) % heads
    lane = lax.broadcasted_iota(jnp.int32, (rows, LANES), 1)
    same_head = (lane % heads) == r_head

    def local(kf, vf, bias_tiles):
        s = lax.dot_general(q, kf, (((1,), (1,)), ((), ())), preferred_element_type=F32)
        s = jnp.concatenate([s[:, t * LANES:(t + 1) * LANES] + bias_tiles[t] for t in range(len(bias_tiles))], axis=1)
        m_blk = jnp.max(s, axis=-1, keepdims=True)
        p = jnp.exp2(s - m_blk)
        return m_blk, jnp.sum(p, axis=-1, keepdims=True), jnp.dot(p.astype(BF16), vf, preferred_element_type=F32)

    def merge(parts):
        m, l, acc = m_ref[...], l_ref[...], acc_ref[...]
        for m_blk, l_blk, o_blk in parts:
            m_new = jnp.maximum(m, m_blk + cq)
            a_old, a_blk = jnp.exp2(m - m_new), jnp.exp2(m_blk + cq - m_new)
            l = a_old * l + a_blk * l_blk
            acc = a_old * acc + a_blk * o_blk
            m = m_new
        m_ref[...], l_ref[...], acc_ref[...] = m, l, acc

    head_mask = jnp.where(same_head, 0.0, NEG_INF).astype(F32)
    tiles_per_page = kbuf.shape[2] * heads // LANES
    parts = []
    for i in range(group):
        kf = kbuf[slot, i].reshape(-1, HEAD_DIM).astype(BF16)
        vf = vbuf[slot, i].reshape(-1, HEAD_DIM).astype(BF16)
        c = c_ref[0, i] * LOG2E
        parts.append(local(kf, vf, [head_mask - c[t:t + 1, :] for t in range(tiles_per_page)]))
    merge(parts)

    @pl.when(g == n_groups - 1)
    def _():
        r_query = lax.broadcasted_iota(jnp.int32, (rows, LANES), 0) // heads
        ok = same_head & ((lane // heads) <= r_query) & (lane < rows)
        bias = jnp.where(ok, 0.0, NEG_INF).astype(F32) - cn_ref[0, 0][0:1, :] * LOG2E
        merge([local(kn_ref[0], vn_ref[0], [bias])])
        o_ref[0] = acc_ref[...] * (1.0 / l_ref[...]) * sg_ref[0].astype(F32)


def _attn_sample_call(page_table, cache_k, cache_v, layer, q_rows, c_all, cq_col, kn_rows, vn_rows, sg_rows,
                      group):
    db, n_pages = page_table.shape
    _, _, page, heads, hd = cache_k.shape
    rows = q_rows.shape[1]
    dec_seq = rows // heads
    n_groups = n_pages // group
    tiles_per_page = page * heads // LANES

    n_slots = RING_SLOTS
    assert db * n_groups >= n_slots - 1
    in_specs = [
        pl.BlockSpec(memory_space=pl.ANY),
        pl.BlockSpec(memory_space=pl.ANY),
        pl.BlockSpec((1, rows, hd), lambda s, g, pt: (s, 0, 0)),
        pl.BlockSpec((1, group, tiles_per_page, LANES), lambda s, g, pt: (s, g, 0, 0)),
        pl.BlockSpec((1, 1, tiles_per_page, LANES), lambda s, g, pt: (s, n_pages, 0, 0)),
        pl.BlockSpec((1, rows, 1), lambda s, g, pt: (s, 0, 0)),
        pl.BlockSpec((1, LANES, hd), lambda s, g, pt: (s, 0, 0)),
        pl.BlockSpec((1, LANES, hd), lambda s, g, pt: (s, 0, 0)),
        pl.BlockSpec((1, rows, hd), lambda s, g, pt: (s, 0, 0)),
    ]
    ring = (n_slots, group, page, heads, hd)
    vmem = 2 * n_slots * group * page * heads * hd * 4 + 4 * rows * page * heads * 4 + (4 << 20)
    return pl.pallas_call(
        functools.partial(_attn_sample_body, layer=layer, group=group, heads=heads, dec_seq=dec_seq,
                          n_groups=n_groups, n_slots=n_slots),
        grid_spec=pltpu.PrefetchScalarGridSpec(
            num_scalar_prefetch=1, grid=(db, n_groups), in_specs=in_specs,
            out_specs=pl.BlockSpec((1, rows, hd), lambda s, g, pt: (s, 0, 0)),
            scratch_shapes=[pltpu.VMEM(ring, F32), pltpu.VMEM(ring, F32), pltpu.SemaphoreType.DMA((2, n_slots)),
                            pltpu.VMEM((rows, 1), F32), pltpu.VMEM((rows, 1), F32), pltpu.VMEM((rows, hd), F32)]),
        out_shape=jax.ShapeDtypeStruct((db, rows, hd), F32),
        compiler_params=_params(vmem, 2),
        name="attn_sample",
    )(page_table, cache_k, cache_v, q_rows, c_all, c_all, cq_col, kn_rows, vn_rows, sg_rows)


def _pick(n, pref):
    t = min(pref, n)
    assert n % t == 0, (n, t)
    return t


def _layer(xp, xs, cache_k, cache_v, logf_pages, state_conv_l, page_table, layer,
           g_pre, w_t, b_f, w_conv, w_out, g_post):
    b, s, d = xp.shape
    db, ds, _ = xs.shape
    heads = b_f.shape[0]
    da = heads * HEAD_DIM
    dc = w_conv.shape[1]
    page = cache_k.shape[2]
    n_pages = page_table.shape[1]
    mp, msz = b * s, db * ds
    scale = HEAD_DIM ** -0.5 * LOG2E

    tc = _pick(dc, 256)
    n_ct = dc // tc
    off = 3 * da + heads
    assert w_t.shape[1] == off + da + 4 * dc and _pick(4 * dc, 256) == tc
    wqkv = _weights_call(w_t, layer, 0, 3 * da, lambda i: i, "weights_qkv")
    wga = _weights_call(w_t, layer, off, da, lambda i: i, "weights_gate")
    w4 = _weights_call(w_t, layer, off + da, 4 * dc, lambda i: 4 * (i % n_ct) + i // n_ct, "weights_conv")
    wf = jnp.pad(w_t[layer, 3 * da:off], ((0, LANES - heads), (0, 0))).astype(BF16)
    bf = jnp.pad(b_f, (0, LANES - heads)).reshape(1, LANES).astype(F32)
    wo = w_out.astype(BF16)
    gpre = g_pre.reshape(1, d)
    gpost = g_post.reshape(1, d)

    def project(x2d, tm, v_transposed):
        xn = _rmsnorm_bf16(x2d, gpre)
        tn = _pick(da, 1024)
        (qv,) = _proj_call(functools.partial(_q_body, scale=scale), xn, wqkv, [BF16], tm, tn, "proj_q", da, 0)
        k32, k16 = _proj_call(_kv_body, xn, wqkv, [F32, BF16], tm, tn, "proj_k", da, da)
        v32, v16 = _proj_call(_vt_body if v_transposed else _kv_body, xn, wqkv, [F32, BF16], tm, tn, "proj_v",
                              da, 2 * da, last_transposed=v_transposed)
        (sg,) = _proj_call(_gate_body, xn, wga, [BF16], tm, tn, "proj_gate")
        return xn, qv, k32, k16, v32, v16, sg

    xp2 = xp.reshape(mp, d)
    tm_p = _pick(s, 512)
    xn, qv, k32, k16, v32, v16t, sg = project(xp2, tm_p, True)
    tm_f = _pick(s, 256)
    logf_pad, c_pad = _forget_call(xn, wf, bf, s // tm_f, tm_f)
    logf_p = logf_pad[:, :heads].reshape(b, s, heads)
    c_p = c_pad[:, :heads].reshape(b, s, heads)
    c_row = c_p.transpose(0, 2, 1).reshape(b, heads, 1, s)
    blk = _pick(s, 512)
    ma_p = _attn_prompt_call(qv.reshape(b, s, da), k16.reshape(b, s, da), v16t,
                             c_row, sg.reshape(b, s, da), blk)
    mc_p, conv_p = _conv_prompt_call(xn, w4, w_conv, b, tm_p, tc)
    tm_o = _pick(s, 512)
    yp = _out_call(ma_p.reshape(mp, da), mc_p, wo, xp2, gpost, tm_o, _pick(d, 512)).reshape(b, s, d)
    k_p = k32.reshape(b, s, heads, HEAD_DIM)
    v_p = v32.reshape(b, s, heads, HEAD_DIM)

    xs2 = xs.reshape(msz, d)
    tm_s = _pick(msz, 512)
    xn, qv, k32, k16, v32, v16, sg = project(xs2, tm_s, False)
    logf_pad, _ = _forget_call(xn, wf, bf, 1, tm_s)
    logf_s = logf_pad[:, :heads].reshape(db, ds, heads)
    rows = ds * heads
    tiles_per_page = page * heads // LANES
    new_rows = jnp.pad(logf_s.reshape(db, 1, rows), ((0, 0), (0, tiles_per_page - 1), (0, LANES - rows)))
    c_all = _cum_call(page_table, logf_pages, layer, new_rows, heads)
    cq_s = c_all[:, n_pages, 0, :rows].reshape(db, rows, 1)
    pad_rows = ((0, 0), (0, LANES - rows), (0, 0))
    kn = jnp.pad(k16.reshape(db, rows, HEAD_DIM), pad_rows)
    vn = jnp.pad(v16.reshape(db, rows, HEAD_DIM), pad_rows)
    o_s = _attn_sample_call(page_table, cache_k, cache_v, layer, qv.reshape(db, rows, HEAD_DIM), c_all, cq_s,
                            kn, vn, sg.reshape(db, rows, HEAD_DIM), _pick(n_pages, 4))
    ma_s = o_s.reshape(msz, da).astype(BF16)
    hist = state_conv_l
    zeros = jnp.zeros((db, 1, dc), F32)
    h1 = jnp.concatenate([hist[:, 1:2]] + [zeros] * (ds - 1), axis=1).reshape(msz, dc)
    h2 = jnp.concatenate([hist[:, 0:1], hist[:, 1:2]] + [zeros] * (ds - 2), axis=1).reshape(msz, dc)
    mc_s, u_s = _conv_sample_call(xn, w4, w_conv, h1, h2, ds, tc)
    ys = _out_call(ma_s, mc_s, wo, xs2, gpost, _pick(msz, 512), _pick(d, 512)).reshape(db, ds, d)
    conv_s = u_s.reshape(db, ds, dc)[:, ds - (CONV_W - 1):]
    k_s = k32.reshape(db, ds, heads, HEAD_DIM)
    v_s = v32.reshape(db, ds, heads, HEAD_DIM)
    return yp, ys, (k_p, v_p, logf_p, conv_p, k_s, v_s, logf_s, conv_s)


def kernel(x_prompt, x_sample, cache_k, cache_v, cache_logf, state_conv, page_table, g_pre, w_in, b_f, w_conv,
           w_out, g_post):
    depth, n_pool, page, heads = cache_logf.shape
    assert cache_k.shape[-1] == HEAD_DIM and (page * heads) % LANES == 0 and LANES % heads == 0
    assert x_sample.shape[1] >= CONV_W - 1 and x_sample.shape[1] * heads <= LANES
    logf_pages = cache_logf.reshape(depth, n_pool, page * heads // LANES, LANES)
    w_t = jnp.swapaxes(w_in, 1, 2)
    xp, xs = x_prompt, x_sample
    per_layer = []
    for layer in range(depth):
        xp, xs, outs = _layer(xp, xs, cache_k, cache_v, logf_pages, state_conv[layer], page_table, layer,
                              g_pre[layer], w_t, b_f[layer], w_conv[layer], w_out[layer], g_post[layer])
        per_layer.append(outs)
    stacked = [jnp.stack(leaf) for leaf in zip(*per_layer)]
    return (xp, xs, *stacked)
```
